```python
import jax, jax.numpy as jnp
from jax import lax
import numpy as np

D_MODEL = 1024
BATCH = 2
SEQ = 8192
DEPTH = 1
DEC_BATCH = 8
DEC_SEQ = 4096
PAST_LEN = 128

D_MIX = D_MODEL
D_LRU = D_MIX // 2
LRU_BLOCKS = 8
LRU_BW = D_LRU // LRU_BLOCKS
CONV_W = 4
LRU_C = 8.0
GLA_HEADS = 4
GLA_DV = (D_MIX - D_LRU) // GLA_HEADS
GLA_DK = GLA_DV // 2
GLA_RANK = 16
GLA_TAU = 16.0
GLA_CHUNK = 16
N_EXPERTS = 32
TOP_K = 4
D_FF = D_MODEL
SWIGLU_LIMIT = 7.0
SWIGLU_ALPHA = 1.702
MOE_BLOCK = 128
EPS = 1e-6
SPLITS = (D_LRU, D_LRU, GLA_HEADS * GLA_DK, GLA_HEADS * GLA_DK,
          GLA_HEADS * GLA_DV, GLA_HEADS * GLA_DV, 2 * GLA_RANK)
D_IN_PROJ = sum(SPLITS)

kernel_name = "hymba_rglru_gla_moe_encoder"


def rms_norm(x, g):
    xf = x.astype(jnp.float32)
    y = xf * lax.rsqrt(jnp.mean(xf * xf, axis=-1, keepdims=True) + EPS)
    return (y * g.astype(jnp.float32)).astype(x.dtype)


def depthwise_conv(x, w, b):
    y = lax.conv_general_dilated(
        x, w[:, None, :], window_strides=(1,),
        padding=[(CONV_W // 2, CONV_W - 1 - CONV_W // 2)],
        dimension_numbers=('NWC', 'WIO', 'NWC'),
        feature_group_count=x.shape[-1])
    return y + b


def block_diag(x, w, b):
    xb = x.reshape(x.shape[:-1] + (LRU_BLOCKS, LRU_BW))
    return jnp.einsum('bshi,hij->bshj', xb, w).reshape(x.shape) + b


def rg_lru_direction(x, w_r, b_r, w_i, b_i, lam, reverse):
    r = jax.nn.sigmoid(block_diag(x, w_r, b_r).astype(jnp.float32))
    i = jax.nn.sigmoid(block_diag(x, w_i, b_i).astype(jnp.float32))
    log_a = LRU_C * r * jax.nn.log_sigmoid(lam.astype(jnp.float32))
    a = jnp.exp(log_a)
    b = jnp.sqrt(-jnp.expm1(2.0 * log_a)) * (i * x.astype(jnp.float32))

    def combine(c1, c2):
        a1, b1 = c1
        a2, b2 = c2
        return a1 * a2, a2 * b1 + b2

    _, h = lax.associative_scan(combine, (a, b), reverse=reverse, axis=1)
    return h


def gla_chunked(q, k, v, log_a, strict):
    B, S, H, DK = q.shape
    DV = v.shape[-1]
    C = GLA_CHUNK
    N = S // C
    q = q.reshape(B, N, C, H, DK)
    k = k.reshape(B, N, C, H, DK)
    v = v.reshape(B, N, C, H, DV)
    bcum = jnp.cumsum(log_a.reshape(B, N, C, H, DK), axis=2)
    idx = jnp.arange(C)
    mask = (idx[:, None] > idx[None, :]) if strict else (idx[:, None] >= idx[None, :])
    diff = bcum[:, :, :, None] - bcum[:, :, None, :]
    decay = jnp.exp(jnp.where(mask[None, None, :, :, None, None], diff, -jnp.inf))
    scores = jnp.einsum('bnihd,bnjhd,bnijhd->bnhij', q, k, decay)
    o_intra = jnp.einsum('bnhij,bnjhv->bnihv', scores, v)
    b_last = bcum[:, :, -1]
    k_dec = k * jnp.exp(b_last[:, :, None] - bcum)
    kv = jnp.einsum('bnchd,bnchv->bnhdv', k_dec, v)

    def step(state, inp):
        dec, kv_n = inp
        return state * dec[..., None] + kv_n, state

    init = jnp.zeros((B, H, DK, DV), jnp.float32)
    _, s_prev = lax.scan(step, init, (jnp.moveaxis(jnp.exp(b_last), 1, 0),
                                      jnp.moveaxis(kv, 1, 0)))
    s_prev = jnp.moveaxis(s_prev, 0, 1)
    o_inter = jnp.einsum('bnchd,bnhdv->bnchv', q * jnp.exp(bcum), s_prev)
    return (o_intra + o_inter).reshape(B, S, H, DV)


def token_mix(xn, w_mix_in, conv_w, conv_b, lru_w_r, lru_b_r, lru_w_i, lru_b_i,
              lru_lambda, gla_w_alpha, gla_b_alpha, gla_norm_g, w_mix_out):
    B, S, _ = xn.shape
    f32 = jnp.float32
    proj = xn @ w_mix_in
    offsets = np.cumsum(SPLITS)[:-1].tolist()
    lru_x, lru_gate, q, k, v, g, a_lr = jnp.split(proj, offsets, axis=-1)
    lru_x = depthwise_conv(lru_x, conv_w, conv_b)
    h = (rg_lru_direction(lru_x, lru_w_r[0], lru_b_r[0], lru_w_i[0], lru_b_i[0], lru_lambda[0], False)
         + rg_lru_direction(lru_x, lru_w_r[1], lru_b_r[1], lru_w_i[1], lru_b_i[1], lru_lambda[1], True))
    lru_out = h * jax.nn.gelu(lru_gate.astype(f32))
    q = q.astype(f32).reshape(B, S, GLA_HEADS, GLA_DK) * (GLA_DK ** -0.5)
    k = k.astype(f32).reshape(B, S, GLA_HEADS, GLA_DK)
    v = v.astype(f32).reshape(B, S, GLA_HEADS, GLA_DV)
    a_f, a_b = jnp.split(a_lr.astype(f32), 2, axis=-1)
    log_a_f = (jax.nn.log_sigmoid(a_f @ gla_w_alpha[0].astype(f32) + gla_b_alpha[0].astype(f32))
               / GLA_TAU).reshape(B, S, GLA_HEADS, GLA_DK)
    log_a_b = (jax.nn.log_sigmoid(a_b @ gla_w_alpha[1].astype(f32) + gla_b_alpha[1].astype(f32))
               / GLA_TAU).reshape(B, S, GLA_HEADS, GLA_DK)
    o_fwd = gla_chunked(q, k, v, log_a_f, False)
    o_bwd = jnp.flip(gla_chunked(jnp.flip(q, 1), jnp.flip(k, 1), jnp.flip(v, 1),
                                 jnp.flip(log_a_b, 1), True), 1)
    o = o_fwd + o_bwd
    o = o * lax.rsqrt(jnp.mean(o * o, axis=-1, keepdims=True) + EPS)
    gla_out = (o.reshape(B, S, GLA_HEADS * GLA_DV) * gla_norm_g.astype(f32)
               * jax.nn.silu(g.astype(f32)))
    cat = jnp.concatenate([lru_out, gla_out], axis=-1).astype(xn.dtype)
    return cat @ w_mix_out


def moe(xn, w_router, b_router, w_exp_in, b_exp_in, w_exp_out, b_exp_out):
    B, S, D = xn.shape
    T = B * S
    xf = xn.reshape(T, D)
    logits = (xf @ w_router + b_router).astype(jnp.float32)
    top_logit, top_idx = lax.top_k(logits, TOP_K)
    gates = jax.nn.softmax(top_logit, axis=-1)
    n_assign = T * TOP_K
    flat_e = top_idx.reshape(-1)
    flat_tok = jnp.arange(n_assign, dtype=jnp.int32) // TOP_K
    flat_w = gates.reshape(-1)
    order = jnp.argsort(flat_e, stable=True)
    sorted_e = flat_e[order]
    counts = jnp.bincount(flat_e, length=N_EXPERTS)
    starts = jnp.cumsum(counts) - counts
    padded = ((counts + MOE_BLOCK - 1) // MOE_BLOCK) * MOE_BLOCK
    pad_ends = jnp.cumsum(padded)
    pad_starts = pad_ends - padded
    dest = pad_starts[sorted_e] + jnp.arange(n_assign) - starts[sorted_e]
    n_blocks = n_assign // MOE_BLOCK + N_EXPERTS
    cap = n_blocks * MOE_BLOCK
    buf_tok = jnp.full((cap,), T, jnp.int32).at[dest].set(flat_tok[order])
    buf_w = jnp.zeros((cap,), jnp.float32).at[dest].set(flat_w[order])
    block_e = jnp.minimum(jnp.searchsorted(pad_ends, jnp.arange(n_blocks) * MOE_BLOCK, side='right'),
                          N_EXPERTS - 1)
    x_pad = jnp.concatenate([xf, jnp.zeros((1, D), xf.dtype)], axis=0)
    xb = x_pad[buf_tok].reshape(n_blocks, MOE_BLOCK, D)

    def expert_block(args):
        xblk, e = args
        hid = xblk @ w_exp_in[e] + b_exp_in[e]
        gate = jnp.minimum(hid[:, ::2], SWIGLU_LIMIT)
        up = jnp.clip(hid[:, 1::2], -SWIGLU_LIMIT, SWIGLU_LIMIT)
        glu = gate * jax.nn.sigmoid(SWIGLU_ALPHA * gate)
        return ((up + 1) * glu) @ w_exp_out[e] + b_exp_out[e]

    yb = lax.map(expert_block, (xb, block_e)).reshape(cap, D)
    yb = yb * buf_w[:, None].astype(yb.dtype)
    y = jnp.zeros((T + 1, D), yb.dtype).at[buf_tok].add(yb)[:T]
    return y.reshape(B, S, D)


def trunk(x, params):
    (mix_norm_g, w_mix_in, lru_conv_w, lru_conv_b, lru_w_r, lru_b_r, lru_w_i, lru_b_i,
     lru_lambda, gla_w_alpha, gla_b_alpha, gla_norm_g, w_mix_out, ffn_norm_g,
     w_router, b_router, w_exp_in, b_exp_in, w_exp_out, b_exp_out, final_norm_g) = params
    for l in range(DEPTH):
        xn = rms_norm(x, mix_norm_g[l])
        x = x + token_mix(xn, w_mix_in[l], lru_conv_w[l], lru_conv_b[l], lru_w_r[l], lru_b_r[l],
                          lru_w_i[l], lru_b_i[l], lru_lambda[l], gla_w_alpha[l], gla_b_alpha[l],
                          gla_norm_g[l], w_mix_out[l]).astype(x.dtype)
        hn = rms_norm(x, ffn_norm_g[l])
        x = x + moe(hn, w_router[l], b_router[l], w_exp_in[l], b_exp_in[l],
                    w_exp_out[l], b_exp_out[l]).astype(x.dtype)
    return rms_norm(x, final_norm_g)


def setup_inputs(seed: int = 0) -> dict:
    key = jax.random.key(seed)
    ks = jax.random.split(key, 24)
    f32 = jnp.float32
    L = DEPTH

    def nrm(k, shape, scale):
        return jax.random.normal(k, shape, f32) * scale

    u = jax.random.uniform(ks[9], (L, 2, D_LRU), f32, 0.9, 0.999)
    a_base = u ** (1.0 / LRU_C)
    lam = jnp.log(a_base) - jnp.log1p(-a_base)
    return {
        "x_prompt": nrm(ks[0], (BATCH, SEQ, D_MODEL), 1.0),
        "x_sample": nrm(ks[1], (DEC_BATCH, DEC_SEQ, D_MODEL), 1.0),
        "mix_norm_g": 1.0 + nrm(ks[2], (L, D_MODEL), 0.02),
        "w_mix_in": nrm(ks[3], (L, D_MODEL, D_IN_PROJ), D_MODEL ** -0.5),
        "lru_conv_w": nrm(ks[4], (L, CONV_W, D_LRU), CONV_W ** -0.5),
        "lru_conv_b": nrm(ks[5], (L, D_LRU), 0.02),
        "lru_w_r": nrm(ks[6], (L, 2, LRU_BLOCKS, LRU_BW, LRU_BW), LRU_BW ** -0.5),
        "lru_b_r": nrm(ks[7], (L, 2, D_LRU), 0.02),
        "lru_w_i": nrm(ks[8], (L, 2, LRU_BLOCKS, LRU_BW, LRU_BW), LRU_BW ** -0.5),
        "lru_b_i": nrm(ks[10], (L, 2, D_LRU), 0.02),
        "lru_lambda": lam,
        "gla_w_alpha": nrm(ks[11], (L, 2, GLA_RANK, GLA_HEADS * GLA_DK), GLA_RANK ** -0.5),
        "gla_b_alpha": nrm(ks[12], (L, 2, GLA_HEADS * GLA_DK), 0.1),
        "gla_norm_g": 1.0 + nrm(ks[13], (L, GLA_HEADS * GLA_DV), 0.02),
        "w_mix_out": nrm(ks[14], (L, D_MIX, D_MODEL), D_MIX ** -0.5),
        "ffn_norm_g": 1.0 + nrm(ks[15], (L, D_MODEL), 0.02),
        "w_router": nrm(ks[16], (L, D_MODEL, N_EXPERTS), D_MODEL ** -0.5),
        "b_router": nrm(ks[17], (L, N_EXPERTS), 0.01),
        "w_exp_in": nrm(ks[18], (L, N_EXPERTS, D_MODEL, 2 * D_FF), D_MODEL ** -0.5),
        "b_exp_in": nrm(ks[19], (L, N_EXPERTS, 2 * D_FF), 0.02),
        "w_exp_out": nrm(ks[20], (L, N_EXPERTS, D_FF, D_MODEL), D_FF ** -0.5),
        "b_exp_out": nrm(ks[21], (L, N_EXPERTS, D_MODEL), 0.02),
        "final_norm_g": 1.0 + nrm(ks[22], (D_MODEL,), 0.02),
    }


def reference(x_prompt, x_sample, mix_norm_g, w_mix_in, lru_conv_w, lru_conv_b, lru_w_r, lru_b_r,
              lru_w_i, lru_b_i, lru_lambda, gla_w_alpha, gla_b_alpha, gla_norm_g, w_mix_out,
              ffn_norm_g, w_router, b_router, w_exp_in, b_exp_in, w_exp_out, b_exp_out,
              final_norm_g):
    params = (mix_norm_g, w_mix_in, lru_conv_w, lru_conv_b, lru_w_r, lru_b_r, lru_w_i, lru_b_i,
              lru_lambda, gla_w_alpha, gla_b_alpha, gla_norm_g, w_mix_out, ffn_norm_g,
              w_router, b_router, w_exp_in, b_exp_in, w_exp_out, b_exp_out, final_norm_g)
    y_prompt = trunk(x_prompt, params)
    y_sample = trunk(x_sample, params)
    return (y_prompt, y_sample)
```

```python
import functools

import jax
import jax.numpy as jnp
from jax import lax
from jax.experimental import pallas as pl
from jax.experimental.pallas import tpu as pltpu

f32 = jnp.float32
bf16 = jnp.bfloat16
i32 = jnp.int32

D_MODEL = 1024
D_LRU = 512
LRU_BLOCKS = 8
LRU_BW = 64
CONV_W = 4
LRU_C = 8.0
GLA_HEADS = 4
GLA_DK = 64
GLA_DV = 128
HK = GLA_HEADS * GLA_DK
HV = GLA_HEADS * GLA_DV
GLA_RANK = 16
GLA_TAU = 16.0
N_EXPERTS = 32
TOP_K = 4
D_FF = 1024
SWIGLU_LIMIT = 7.0
SWIGLU_ALPHA = 1.702
EPS = 1e-6

LANES = 128
SUBLANES = 8
D_IN_PAD = 2688
VMEM_LIMIT = 56 * 1024 * 1024

TM_PROJ = 512
L_MIX = 128
BM_EXP = 256


def _cparams(n_axes):
    return pltpu.CompilerParams(
        dimension_semantics=("arbitrary",) * n_axes, vmem_limit_bytes=VMEM_LIMIT)


def _rms(x, g):
    ms = jnp.mean(x * x, axis=-1, keepdims=True)
    return x * lax.rsqrt(ms + EPS) * g


def _sigmoid(x):
    return 1.0 / (1.0 + jnp.exp(-x))


def _log_sigmoid(x):
    return jnp.minimum(x, 0.0) - jnp.log1p(jnp.exp(-jnp.abs(x)))


def _in_proj_body(tiles_per_seq, x_ref, xprev_ref, xnext_ref, g_ref, w_ref, cw_ref, cb_ref,
                  u_ref, gate_ref, q_ref, k_ref, v_ref, g2_ref, a_ref, ext_ref):
    i = pl.program_id(0)
    tm = x_ref.shape[0]
    g = g_ref[...]
    xn = _rms(x_ref[...], g).astype(bf16)

    def proj(lo, hi):
        return jnp.dot(xn, w_ref[:, lo:hi], preferred_element_type=f32)

    gate_ref[...] = proj(512, 1024).astype(bf16)
    q_ref[...] = proj(1024, 1280).astype(bf16)
    k_ref[...] = proj(1280, 1536).astype(bf16)
    v_ref[...] = proj(1536, 2048).astype(bf16)
    g2_ref[...] = proj(2048, 2560).astype(bf16)
    a_ref[...] = proj(2560, 2688)[:, :2 * GLA_RANK]

    xh = jnp.concatenate([xprev_ref[...], xnext_ref[...]], axis=0)
    hl = jnp.dot(_rms(xh, g).astype(bf16), w_ref[:, 0:512], preferred_element_type=f32)
    pos = i % tiles_per_seq
    ext_ref[0:SUBLANES, :] = jnp.where(pos == 0, 0.0, hl[:SUBLANES])
    ext_ref[SUBLANES:SUBLANES + tm, :] = proj(0, 512)
    ext_ref[SUBLANES + tm:, :] = jnp.where(pos == tiles_per_seq - 1, 0.0, hl[SUBLANES:])
    u = cb_ref[...]
    for j in range(CONV_W):
        u = u + cw_ref[j:j + 1, :] * ext_ref[pl.ds(SUBLANES - CONV_W // 2 + j, tm), :]
    u_ref[...] = u


def _in_proj(x2, seq, g, w, cw, cb):
    t = x2.shape[0]
    tm = min(TM_PROJ, seq)
    nt = t // tm
    r = tm // SUBLANES
    last_blk = t // SUBLANES - 1
    row = lambda i: (i, 0)
    const = lambda i: (0, 0)
    out_shapes = [
        jax.ShapeDtypeStruct((t, D_LRU), f32),
        jax.ShapeDtypeStruct((t, D_LRU), bf16),
        jax.ShapeDtypeStruct((t, HK), bf16),
        jax.ShapeDtypeStruct((t, HK), bf16),
        jax.ShapeDtypeStruct((t, HV), bf16),
        jax.ShapeDtypeStruct((t, HV), bf16),
        jax.ShapeDtypeStruct((t, 2 * GLA_RANK), f32),
    ]
    return pl.pallas_call(
        functools.partial(_in_proj_body, seq // tm),
        grid=(nt,),
        in_specs=[
            pl.BlockSpec((tm, D_MODEL), row),
            pl.BlockSpec((SUBLANES, D_MODEL), lambda i: (jnp.maximum(i * r - 1, 0), 0)),
            pl.BlockSpec((SUBLANES, D_MODEL), lambda i: (jnp.minimum((i + 1) * r, last_blk), 0)),
            pl.BlockSpec((1, D_MODEL), const),
            pl.BlockSpec((D_MODEL, D_IN_PAD), const),
            pl.BlockSpec((CONV_W, D_LRU), const),
            pl.BlockSpec((1, D_LRU), const),
        ],
        out_specs=[pl.BlockSpec((tm, s.shape[1]), row) for s in out_shapes],
        out_shape=out_shapes,
        scratch_shapes=[pltpu.VMEM((tm + 2 * SUBLANES, D_LRU), f32)],
        compiler_params=_cparams(1),
        name="in_proj",
    )(x2, x2, x2, g, w, cw, cb)


def _lru_tile(u_ref, d, reverse, wg_ref, bg_ref, lam_ref, a_scr, b_scr, hcar_ref, h_out_ref):
    n = u_ref.shape[0]
    u = u_ref[...]
    gates = jnp.dot(u.astype(bf16), wg_ref[d], preferred_element_type=f32) + bg_ref[d]
    r = _sigmoid(gates[:, :D_LRU])
    ig = _sigmoid(gates[:, D_LRU:])
    log_a = r * (LRU_C * _log_sigmoid(lam_ref[d]))
    a = jnp.exp(log_a)
    a_scr[...] = a
    b_scr[...] = jnp.sqrt(1.0 - a * a) * (ig * u)

    row = lax.broadcasted_iota(i32, (SUBLANES, D_LRU), 0)
    n_groups = n // SUBLANES

    def group(gi, carry):
        gidx = (n_groups - 1 - gi) if reverse else gi
        off = pl.multiple_of(gidx * SUBLANES, SUBLANES)
        a = a_scr[pl.ds(off, SUBLANES), :]
        b = b_scr[pl.ds(off, SUBLANES), :]
        for s in (1, 2, 4):
            if reverse:
                keep = row < SUBLANES - s
                shift = SUBLANES - s
            else:
                keep = row >= s
                shift = s
            a_nb = jnp.where(keep, pltpu.roll(a, shift, 0), 1.0)
            b_nb = jnp.where(keep, pltpu.roll(b, shift, 0), 0.0)
            b = a * b_nb + b
            a = a * a_nb
        h = a * carry + b
        a_scr[pl.ds(off, SUBLANES), :] = h
        edge = h[0:1, :] if reverse else h[SUBLANES - 1:SUBLANES, :]
        return jnp.broadcast_to(edge, (SUBLANES, D_LRU))

    hcar_ref[d] = lax.fori_loop(0, n_groups, group, hcar_ref[d])
    h_out_ref[...] = a_scr[...].astype(h_out_ref.dtype)


def _gla_tile(q_ref, k_ref, v_ref, al_ref, d, reverse, wa_ref, ba_ref, s_ref, o_out_ref):
    n = q_ref.shape[0]
    z = jnp.dot(al_ref[...], wa_ref[d], preferred_element_type=f32,
                precision=lax.Precision.HIGHEST) + ba_ref[d]
    la = _log_sigmoid(z) * (1.0 / GLA_TAU)
    la_hi = la.astype(bf16)
    la_lo = (la - la_hi.astype(f32)).astype(bf16)
    ri = lax.broadcasted_iota(i32, (n, n), 0)
    ci = lax.broadcasted_iota(i32, (n, n), 1)
    incl = (ri <= ci) if reverse else (ri >= ci)
    tri = jnp.where(incl, 1.0, 0.0).astype(bf16)
    cum = (jnp.dot(tri, la_hi, preferred_element_type=f32)
           + jnp.dot(tri, la_lo, preferred_element_type=f32))
    tot = cum[0:1, :] if reverse else cum[n - 1:n, :]

    q = q_ref[...].astype(f32)
    k = k_ref[...].astype(f32)
    v = v_ref[...]
    qa = q * (jnp.exp(cum) * (GLA_DK ** -0.5))
    kb = (k * jnp.exp(-cum)).astype(bf16)
    ke_t = (k * jnp.exp(tot - cum)).T.astype(bf16)

    state = s_ref[d]
    o_inter = jnp.dot(qa.astype(bf16), state.astype(bf16), preferred_element_type=f32)
    causal = (ri < ci) if reverse else incl
    head_of_lane = lax.broadcasted_iota(i32, (1, HK), 1) // GLA_DK
    for h in range(GLA_HEADS):
        qh = jnp.where(head_of_lane == h, qa, 0.0).astype(bf16)
        s = lax.dot_general(qh, kb, (((1,), (1,)), ((), ())), preferred_element_type=f32)
        p = jnp.where(causal, s, 0.0).astype(bf16)
        lo, hi = h * GLA_DV, (h + 1) * GLA_DV
        o_h = jnp.dot(p, v[:, lo:hi], preferred_element_type=f32) + o_inter[:, lo:hi]
        o_out_ref[:, lo:hi] = o_h.astype(o_out_ref.dtype)

    kv = jnp.dot(ke_t, v, preferred_element_type=f32)
    dec = jnp.exp(jnp.sum(la.T, axis=1, keepdims=True))
    on_diag = (lax.broadcasted_iota(i32, (HK, HV), 0) // GLA_DK
               == lax.broadcasted_iota(i32, (HK, HV), 1) // GLA_DV)
    s_ref[d] = state * dec + jnp.where(on_diag, kv, 0.0)


def _mix_body(uf_ref, qf_ref, kf_ref, vf_ref, af_ref, ub_ref, qb_ref, kb_ref, vb_ref, ab_ref,
              wg_ref, bg_ref, lam_ref, wa_ref, ba_ref,
              hf_ref, hb_ref, of_ref, ob_ref,
              a_scr, b_scr, hcar_ref, s_ref):
    @pl.when(pl.program_id(1) == 0)
    def _():
        hcar_ref[...] = jnp.zeros_like(hcar_ref)
        s_ref[...] = jnp.zeros_like(s_ref)

    _lru_tile(uf_ref, 0, False, wg_ref, bg_ref, lam_ref, a_scr, b_scr, hcar_ref, hf_ref)
    _lru_tile(ub_ref, 1, True, wg_ref, bg_ref, lam_ref, a_scr, b_scr, hcar_ref, hb_ref)
    _gla_tile(qf_ref, kf_ref, vf_ref, af_ref, 0, False, wa_ref, ba_ref, s_ref, of_ref)
    _gla_tile(qb_ref, kb_ref, vb_ref, ab_ref, 1, True, wa_ref, ba_ref, s_ref, ob_ref)


def _mix(u, q, k, v, al, wg, bg, lam, wa, ba):
    b, s, _ = u.shape
    n = min(L_MIX, s)
    nt = s // n
    fwd = lambda bi, j: (bi, j, 0)
    bwd = lambda bi, j: (bi, nt - 1 - j, 0)
    const3 = lambda bi, j: (0, 0, 0)

    def tile_specs(imap):
        return [pl.BlockSpec((None, n, c), imap) for c in (D_LRU, HK, HK, HV, 2 * GLA_RANK)]

    out_shape = [jax.ShapeDtypeStruct((b, s, c), bf16) for c in (D_LRU, D_LRU, HV, HV)]
    return pl.pallas_call(
        _mix_body,
        grid=(b, nt),
        in_specs=tile_specs(fwd) + tile_specs(bwd) + [
            pl.BlockSpec(wg.shape, const3),
            pl.BlockSpec(bg.shape, const3),
            pl.BlockSpec(lam.shape, const3),
            pl.BlockSpec(wa.shape, const3),
            pl.BlockSpec(ba.shape, const3),
        ],
        out_specs=[
            pl.BlockSpec((None, n, D_LRU), fwd),
            pl.BlockSpec((None, n, D_LRU), bwd),
            pl.BlockSpec((None, n, HV), fwd),
            pl.BlockSpec((None, n, HV), bwd),
        ],
        out_shape=out_shape,
        scratch_shapes=[
            pltpu.VMEM((n, D_LRU), f32),
            pltpu.VMEM((n, D_LRU), f32),
            pltpu.VMEM((2, SUBLANES, D_LRU), f32),
            pltpu.VMEM((2, HK, HV), f32),
        ],
        compiler_params=_cparams(2),
        name="mix",
    )(u, q, k, v, al, u, q, k, v, al, wg, bg, lam, wa, ba)


def _out_proj_body(hf_ref, hb_ref, gate_ref, of_ref, ob_ref, g2_ref, x_ref,
                   wout_ref, gn_ref, fg_ref, wr_ref, br_ref,
                   x1_ref, hn_ref, ri_ref, rg_ref, cnt_ref, cnt_scr):
    i = pl.program_id(0)
    tm = x_ref.shape[0]

    @pl.when(i == 0)
    def _():
        cnt_scr[...] = jnp.zeros_like(cnt_scr)

    hs = hf_ref[...].astype(f32) + hb_ref[...].astype(f32)
    gt = gate_ref[...].astype(f32)
    gelu = 0.5 * gt * (1.0 + jnp.tanh(0.7978845608028654 * (gt + 0.044715 * (gt * gt * gt))))
    lru_out = (hs * gelu).astype(bf16)

    o = of_ref[...].astype(f32) + ob_ref[...].astype(f32)
    g2 = g2_ref[...].astype(f32)
    silu = g2 * _sigmoid(g2)
    mix = jnp.dot(lru_out, wout_ref[0:D_LRU, :], preferred_element_type=f32)
    for h in range(GLA_HEADS):
        lo, hi = h * GLA_DV, (h + 1) * GLA_DV
        oh = o[:, lo:hi]
        on = oh * lax.rsqrt(jnp.mean(oh * oh, axis=-1, keepdims=True) + EPS)
        gla_h = (on * gn_ref[:, lo:hi] * silu[:, lo:hi]).astype(bf16)
        mix = mix + jnp.dot(gla_h, wout_ref[D_LRU + lo:D_LRU + hi, :],
                            preferred_element_type=f32)
    x1 = x_ref[...] + mix
    x1_ref[...] = x1
    hn = _rms(x1, fg_ref[...])
    hn_ref[...] = hn

    logits = jnp.dot(hn, wr_ref[...], preferred_element_type=f32,
                     precision=lax.Precision.HIGHEST) + br_ref[...]
    lane = lax.broadcasted_iota(i32, (tm, LANES), 1)
    lane_f = lane.astype(f32)
    sel_idx, sel_val = [], []
    member = jnp.zeros((tm, LANES), f32)
    for _ in range(TOP_K):
        m = jnp.max(logits, axis=-1, keepdims=True)
        idx = jnp.min(jnp.where(logits == m, lane_f, float(LANES)), axis=-1, keepdims=True)
        hit = lane_f == idx
        member = jnp.where(hit, 1.0, member)
        logits = jnp.where(hit, -jnp.inf, logits)
        sel_idx.append(idx)
        sel_val.append(m)
    ex = [jnp.exp(mv - sel_val[0]) for mv in sel_val]
    inv = 1.0 / (ex[0] + ex[1] + ex[2] + ex[3])

    ri = lax.broadcasted_iota(i32, (tm, tm), 0)
    ci = lax.broadcasted_iota(i32, (tm, tm), 1)
    before = jnp.where(ri > ci, 1.0, 0.0).astype(bf16)
    prefix = jnp.dot(before, member.astype(bf16), preferred_element_type=f32) + cnt_scr[...]
    cnt_new = cnt_scr[...] + jnp.sum(member, axis=0, keepdims=True)
    cnt_scr[...] = cnt_new
    cnt_ref[...] = cnt_new.astype(i32)

    route_i = jnp.zeros((tm, LANES), f32)
    route_g = jnp.zeros((tm, LANES), f32)
    for kk in range(TOP_K):
        rank = jnp.sum(jnp.where(lane_f == sel_idx[kk], prefix, 0.0), axis=-1, keepdims=True)
        route_i = jnp.where(lane == kk, sel_idx[kk], route_i)
        route_i = jnp.where(lane == TOP_K + kk, rank, route_i)
        route_g = jnp.where(lane == kk, ex[kk] * inv, route_g)
    ri_ref[...] = route_i.astype(i32)
    rg_ref[...] = route_g


def _out_proj(hf, hb, gate, of, ob, g2, x2, wout, gn, fg, wr, br):
    t = x2.shape[0]
    tm = min(TM_PROJ, t)
    row = lambda i: (i, 0)
    const = lambda i: (0, 0)
    out_shape = [
        jax.ShapeDtypeStruct((t, D_MODEL), f32),
        jax.ShapeDtypeStruct((t, D_MODEL), f32),
        jax.ShapeDtypeStruct((t, LANES), i32),
        jax.ShapeDtypeStruct((t, LANES), f32),
        jax.ShapeDtypeStruct((1, LANES), i32),
    ]
    return pl.pallas_call(
        _out_proj_body,
        grid=(t // tm,),
        in_specs=[
            pl.BlockSpec((tm, D_LRU), row),
            pl.BlockSpec((tm, D_LRU), row),
            pl.BlockSpec((tm, D_LRU), row),
            pl.BlockSpec((tm, HV), row),
            pl.BlockSpec((tm, HV), row),
            pl.BlockSpec((tm, HV), row),
            pl.BlockSpec((tm, D_MODEL), row),
            pl.BlockSpec((D_MODEL, D_MODEL), const),
            pl.BlockSpec((1, HV), const),
            pl.BlockSpec((1, D_MODEL), const),
            pl.BlockSpec((D_MODEL, LANES), const),
            pl.BlockSpec((1, LANES), const),
        ],
        out_specs=[
            pl.BlockSpec((tm, D_MODEL), row),
            pl.BlockSpec((tm, D_MODEL), row),
            pl.BlockSpec((tm, LANES), row),
            pl.BlockSpec((tm, LANES), row),
            pl.BlockSpec((1, LANES), const),
        ],
        out_shape=out_shape,
        scratch_shapes=[pltpu.VMEM((1, LANES), f32)],
        compiler_params=_cparams(1),
        name="out_proj",
    )(hf, hb, gate, of, ob, g2, x2, wout, gn, fg, wr, br)


def _row_copy(src, src_row, dst, dst_row, sem):
    return pltpu.make_async_copy(src.at[pl.ds(src_row, 1)], dst.at[pl.ds(dst_row, 1)], sem)


def _experts_body(be_ref, nreal_ref, idx_hbm, hn_hbm, wg_ref, wu_ref, bgt_ref, bup_ref,
                  wo_ref, bo_ref, y4_hbm, idx_smem, xbuf, ybuf, isem, gsem, ssem):
    n = pl.program_id(0)
    bm = xbuf.shape[0]
    n_real = nreal_ref[n]

    @pl.when(n_real > 0)
    def _():
        icopy = pltpu.make_async_copy(idx_hbm.at[n], idx_smem, isem)
        icopy.start()
        icopy.wait()
        for r in range(bm):
            _row_copy(hn_hbm, idx_smem[r], xbuf, r, gsem).start()
        pltpu.make_async_copy(hn_hbm.at[pl.ds(0, bm)], xbuf, gsem).wait()

        xb = xbuf[...].astype(bf16)
        gate = jnp.dot(xb, wg_ref[...], preferred_element_type=f32) + bgt_ref[...]
        up = jnp.dot(xb, wu_ref[...], preferred_element_type=f32) + bup_ref[...]
        gate = jnp.minimum(gate, SWIGLU_LIMIT)
        up = jnp.clip(up, -SWIGLU_LIMIT, SWIGLU_LIMIT)
        glu = gate * _sigmoid(SWIGLU_ALPHA * gate)
        act = ((up + 1.0) * glu).astype(bf16)
        ybuf[...] = jnp.dot(act, wo_ref[...], preferred_element_type=f32) + bo_ref[...]

        for r in range(bm):
            @pl.when(r < n_real)
            def _():
                _row_copy(ybuf, r, y4_hbm, idx_smem[bm + r], ssem).start()
        n_whole = pl.multiple_of((n_real // SUBLANES) * SUBLANES, SUBLANES)

        @pl.when(n_whole > 0)
        def _():
            pltpu.make_async_copy(ybuf.at[pl.ds(0, n_whole)], y4_hbm.at[pl.ds(0, n_whole)],
                                  ssem).wait()
        for r in range(SUBLANES - 1):
            @pl.when(n_whole + r < n_real)
            def _():
                _row_copy(ybuf, 0, y4_hbm, 0, ssem).wait()


def _experts(block_e, n_real, idx, hn, wg, wu, bgt, bup, wo, bo, y4_rows):
    n_blocks, two_bm = idx.shape
    bm = two_bm // 2
    wmap = lambda n, be, nu: (be[n], 0, 0)
    grid_spec = pltpu.PrefetchScalarGridSpec(
        num_scalar_prefetch=2,
        grid=(n_blocks,),
        in_specs=[
            pl.BlockSpec(memory_space=pl.ANY),
            pl.BlockSpec(memory_space=pl.ANY),
            pl.BlockSpec((None, D_MODEL, D_FF), wmap),
            pl.BlockSpec((None, D_MODEL, D_FF), wmap),
            pl.BlockSpec((None, 1, D_FF), wmap),
            pl.BlockSpec((None, 1, D_FF), wmap),
            pl.BlockSpec((None, D_FF, D_MODEL), wmap),
            pl.BlockSpec((None, 1, D_MODEL), wmap),
        ],
        out_specs=pl.BlockSpec(memory_space=pl.ANY),
        scratch_shapes=[
            pltpu.SMEM((two_bm,), i32),
            pltpu.VMEM((bm, D_MODEL), f32),
            pltpu.VMEM((bm, D_MODEL), f32),
            pltpu.SemaphoreType.DMA,
            pltpu.SemaphoreType.DMA,
            pltpu.SemaphoreType.DMA,
        ],
    )
    return pl.pallas_call(
        _experts_body,
        grid_spec=grid_spec,
        out_shape=jax.ShapeDtypeStruct((y4_rows, D_MODEL), f32),
        compiler_params=_cparams(1),
        name="experts",
    )(block_e, n_real, idx, hn, wg, wu, bgt, bup, wo, bo)


def _combine_body(y4_ref, rg_ref, x1_ref, g_ref, out_ref):
    acc = x1_ref[...]
    for kk in range(TOP_K):
        acc = acc + rg_ref[:, kk:kk + 1] * y4_ref[:, kk * D_MODEL:(kk + 1) * D_MODEL]
    out_ref[...] = _rms(acc, g_ref[...])


def _combine(y4, rg, x1, g):
    t = x1.shape[0]
    tm = min(TM_PROJ, t)
    row = lambda i: (i, 0)
    return pl.pallas_call(
        _combine_body,
        grid=(t // tm,),
        in_specs=[
            pl.BlockSpec((tm, TOP_K * D_MODEL), row),
            pl.BlockSpec((tm, LANES), row),
            pl.BlockSpec((tm, D_MODEL), row),
            pl.BlockSpec((1, D_MODEL), lambda i: (0, 0)),
        ],
        out_specs=pl.BlockSpec((tm, D_MODEL), row),
        out_shape=jax.ShapeDtypeStruct((t, D_MODEL), f32),
        compiler_params=_cparams(1),
        name="combine",
    )(y4, rg, x1, g)


def _block_diag_dense(w):
    eye = jnp.eye(LRU_BLOCKS, dtype=w.dtype)
    return jnp.einsum('hij,hg->higj', w, eye).reshape(D_LRU, D_LRU)


def _prep(mix_norm_g, w_mix_in, lru_conv_w, lru_conv_b, lru_w_r, lru_b_r, lru_w_i, lru_b_i,
          lru_lambda, gla_w_alpha, gla_b_alpha, gla_norm_g, w_mix_out, ffn_norm_g,
          w_router, b_router, w_exp_in, b_exp_in, w_exp_out, b_exp_out, final_norm_g):
    p = {}
    p["mix_g"] = mix_norm_g[0].reshape(1, D_MODEL)
    p["w_in"] = jnp.pad(w_mix_in[0], ((0, 0), (0, D_IN_PAD - w_mix_in.shape[-1]))).astype(bf16)
    p["conv_w"] = lru_conv_w[0]
    p["conv_b"] = lru_conv_b[0].reshape(1, D_LRU)
    p["wg"] = jnp.stack([
        jnp.concatenate([_block_diag_dense(lru_w_r[0, d]), _block_diag_dense(lru_w_i[0, d])], axis=1)
        for d in range(2)]).astype(bf16)
    p["bg"] = jnp.concatenate([lru_b_r[0], lru_b_i[0]], axis=-1).reshape(2, 1, 2 * D_LRU)
    p["lam"] = lru_lambda[0].reshape(2, 1, D_LRU)
    zeros = jnp.zeros((GLA_RANK, HK), f32)
    p["wa"] = jnp.stack([jnp.concatenate([gla_w_alpha[0, 0], zeros], axis=0),
                         jnp.concatenate([zeros, gla_w_alpha[0, 1]], axis=0)])
    p["ba"] = gla_b_alpha[0].reshape(2, 1, HK)
    p["gn"] = gla_norm_g[0].reshape(1, HV)
    p["w_out"] = w_mix_out[0].astype(bf16)
    p["ffn_g"] = ffn_norm_g[0].reshape(1, D_MODEL)
    p["w_r"] = jnp.pad(w_router[0], ((0, 0), (0, LANES - N_EXPERTS)))
    p["b_r"] = jnp.pad(b_router[0], (0, LANES - N_EXPERTS), constant_values=-1e30).reshape(1, LANES)
    p["w_gate"] = w_exp_in[0, :, :, 0::2].astype(bf16)
    p["w_up"] = w_exp_in[0, :, :, 1::2].astype(bf16)
    p["b_gate"] = b_exp_in[0, :, 0::2].reshape(N_EXPERTS, 1, D_FF)
    p["b_up"] = b_exp_in[0, :, 1::2].reshape(N_EXPERTS, 1, D_FF)
    p["w_eo"] = w_exp_out[0].astype(bf16)
    p["b_eo"] = b_exp_out[0].reshape(N_EXPERTS, 1, D_MODEL)
    p["final_g"] = final_norm_g.reshape(1, D_MODEL)
    return p


def _route_tables(route_i, counts, t):
    bm = BM_EXP
    n_assign = t * TOP_K
    n_blocks = n_assign // bm + N_EXPERTS
    cap = n_blocks * bm
    cnt = counts[0, :N_EXPERTS]
    padded = ((cnt + bm - 1) // bm) * bm
    pad_ends = jnp.cumsum(padded)
    pad_starts = pad_ends - padded
    e = route_i[:, :TOP_K]
    rank = route_i[:, TOP_K:2 * TOP_K]
    pos = (pad_starts[e] + rank).reshape(-1)
    slot_of_row = jnp.full((cap,), -1, i32).at[pos].set(jnp.arange(n_assign, dtype=i32))
    real = slot_of_row >= 0
    src_tok = jnp.where(real, slot_of_row // TOP_K, 0)
    dst_row = jnp.where(real, slot_of_row, 0)
    idx = jnp.concatenate([src_tok.reshape(n_blocks, bm), dst_row.reshape(n_blocks, bm)], axis=1)
    block_start = jnp.arange(n_blocks, dtype=i32) * bm
    block_e = jnp.minimum(jnp.searchsorted(pad_ends, block_start, side='right'),
                          N_EXPERTS - 1).astype(i32)
    n_real = jnp.clip(cnt[block_e] - (block_start - pad_starts[block_e]), 0, bm).astype(i32)
    return block_e, n_real, idx


def _trunk(x, p):
    b, s, _ = x.shape
    t = b * s
    x2 = x.reshape(t, D_MODEL)
    u, gate, q, k, v, g2, al = _in_proj(x2, s, p["mix_g"], p["w_in"], p["conv_w"], p["conv_b"])
    r3 = lambda a: a.reshape(b, s, a.shape[-1])
    hf, hb, of, ob = _mix(r3(u), r3(q), r3(k), r3(v), r3(al),
                          p["wg"], p["bg"], p["lam"], p["wa"], p["ba"])
    f2 = lambda a: a.reshape(t, a.shape[-1])
    x1, hn, route_i, route_g, counts = _out_proj(
        f2(hf), f2(hb), gate, f2(of), f2(ob), g2, x2,
        p["w_out"], p["gn"], p["ffn_g"], p["w_r"], p["b_r"])
    block_e, n_real, idx = _route_tables(route_i, counts, t)
    y4 = _experts(block_e, n_real, idx, hn, p["w_gate"], p["w_up"], p["b_gate"], p["b_up"],
                  p["w_eo"], p["b_eo"], t * TOP_K)
    y = _combine(y4.reshape(t, TOP_K * D_MODEL), route_g, x1, p["final_g"])
    return y.reshape(b, s, D_MODEL)


def kernel(x_prompt, x_sample, mix_norm_g, w_mix_in, lru_conv_w, lru_conv_b, lru_w_r, lru_b_r,
           lru_w_i, lru_b_i, lru_lambda, gla_w_alpha, gla_b_alpha, gla_norm_g, w_mix_out,
           ffn_norm_g, w_router, b_router, w_exp_in, b_exp_in, w_exp_out, b_exp_out,
           final_norm_g):
    p = _prep(mix_norm_g, w_mix_in, lru_conv_w, lru_conv_b, lru_w_r, lru_b_r, lru_w_i, lru_b_i,
              lru_lambda, gla_w_alpha, gla_b_alpha, gla_norm_g, w_mix_out, ffn_norm_g,
              w_router, b_router, w_exp_in, b_exp_in, w_exp_out, b_exp_out, final_norm_g)
    return (_trunk(x_prompt, p), _trunk(x_sample, p))
```

```python
import functools

import jax
import jax.numpy as jnp
from jax import lax
from jax.experimental import pallas as pl
from jax.experimental.pallas import tpu as pltpu

f32 = jnp.float32
bf16 = jnp.bfloat16
i32 = jnp.int32

D_MODEL = 1024
D_LRU = 512
LRU_BLOCKS = 8
LRU_BW = 64
CONV_W = 4
LRU_C = 8.0
GLA_HEADS = 4
GLA_DK = 64
GLA_DV = 128
HK = GLA_HEADS * GLA_DK
HV = GLA_HEADS * GLA_DV
GLA_RANK = 16
GLA_TAU = 16.0
N_EXPERTS = 32
TOP_K = 4
D_FF = 1024
SWIGLU_LIMIT = 7.0
SWIGLU_ALPHA = 1.702
EPS = 1e-6

LANES = 128
SUBLANES = 8
D_IN_PAD = 2688
VMEM_LIMIT = 56 * 1024 * 1024

TM_PROJ = 512
L_MIX = 128
BM_EXP = 256


def _cparams(n_axes):
    return pltpu.CompilerParams(
        dimension_semantics=("arbitrary",) * n_axes, vmem_limit_bytes=VMEM_LIMIT)


def _rms(x, g):
    ms = jnp.mean(x * x, axis=-1, keepdims=True)
    return x * lax.rsqrt(ms + EPS) * g


def _sigmoid(x):
    return 1.0 / (1.0 + jnp.exp(-x))


def _log_sigmoid(x):
    return jnp.minimum(x, 0.0) - jnp.log1p(jnp.exp(-jnp.abs(x)))


def _in_proj_body(tiles_per_seq, x_ref, xprev_ref, xnext_ref, g_ref, w_ref, cw_ref, cb_ref,
                  u_ref, gate_ref, q_ref, k_ref, v_ref, g2_ref, a_ref, ext_ref):
    i = pl.program_id(0)
    tm = x_ref.shape[0]
    g = g_ref[...]
    xn = _rms(x_ref[...], g).astype(bf16)

    def proj(lo, hi):
        return jnp.dot(xn, w_ref[:, lo:hi], preferred_element_type=f32)

    gate_ref[...] = proj(512, 1024).astype(bf16)
    q_ref[...] = proj(1024, 1280).astype(bf16)
    k_ref[...] = proj(1280, 1536).astype(bf16)
    v_ref[...] = proj(1536, 2048).astype(bf16)
    g2_ref[...] = proj(2048, 2560).astype(bf16)
    a_ref[...] = proj(2560, 2688)[:, :2 * GLA_RANK]

    xh = jnp.concatenate([xprev_ref[...], xnext_ref[...]], axis=0)
    hl = jnp.dot(_rms(xh, g).astype(bf16), w_ref[:, 0:512], preferred_element_type=f32)
    pos = i % tiles_per_seq
    ext_ref[0:SUBLANES, :] = jnp.where(pos == 0, 0.0, hl[:SUBLANES])
    ext_ref[SUBLANES:SUBLANES + tm, :] = proj(0, 512)
    ext_ref[SUBLANES + tm:, :] = jnp.where(pos == tiles_per_seq - 1, 0.0, hl[SUBLANES:])
    u = cb_ref[...]
    for j in range(CONV_W):
        u = u + cw_ref[j:j + 1, :] * ext_ref[pl.ds(SUBLANES - CONV_W // 2 + j, tm), :]
    u_ref[...] = u


def _in_proj(x2, seq, g, w, cw, cb):
    t = x2.shape[0]
    tm = min(TM_PROJ, seq)
    nt = t // tm
    r = tm // SUBLANES
    last_blk = t // SUBLANES - 1
    row = lambda i: (i, 0)
    const = lambda i: (0, 0)
    out_shapes = [
        jax.ShapeDtypeStruct((t, D_LRU), f32),
        jax.ShapeDtypeStruct((t, D_LRU), bf16),
        jax.ShapeDtypeStruct((t, HK), bf16),
        jax.ShapeDtypeStruct((t, HK), bf16),
        jax.ShapeDtypeStruct((t, HV), bf16),
        jax.ShapeDtypeStruct((t, HV), bf16),
        jax.ShapeDtypeStruct((t, 2 * GLA_RANK), f32),
    ]
    return pl.pallas_call(
        functools.partial(_in_proj_body, seq // tm),
        grid=(nt,),
        in_specs=[
            pl.BlockSpec((tm, D_MODEL), row),
            pl.BlockSpec((SUBLANES, D_MODEL), lambda i: (jnp.maximum(i * r - 1, 0), 0)),
            pl.BlockSpec((SUBLANES, D_MODEL), lambda i: (jnp.minimum((i + 1) * r, last_blk), 0)),
            pl.BlockSpec((1, D_MODEL), const),
            pl.BlockSpec((D_MODEL, D_IN_PAD), const),
            pl.BlockSpec((CONV_W, D_LRU), const),
            pl.BlockSpec((1, D_LRU), const),
        ],
        out_specs=[pl.BlockSpec((tm, s.shape[1]), row) for s in out_shapes],
        out_shape=out_shapes,
        scratch_shapes=[pltpu.VMEM((tm + 2 * SUBLANES, D_LRU), f32)],
        compiler_params=_cparams(1),
        name="in_proj",
    )(x2, x2, x2, g, w, cw, cb)


def _lru_tile(u_ref, d, reverse, wg_ref, bg_ref, lam_ref, a_scr, b_scr, hcar_ref, h_out_ref):
    n = u_ref.shape[0]
    u = u_ref[...]
    gates = jnp.dot(u.astype(bf16), wg_ref[d], preferred_element_type=f32) + bg_ref[d]
    r = _sigmoid(gates[:, :D_LRU])
    ig = _sigmoid(gates[:, D_LRU:])
    log_a = r * (LRU_C * _log_sigmoid(lam_ref[d]))
    a = jnp.exp(log_a)
    a_scr[...] = a
    b_scr[...] = jnp.sqrt(1.0 - a * a) * (ig * u)

    row = lax.broadcasted_iota(i32, (SUBLANES, D_LRU), 0)
    n_groups = n // SUBLANES

    def group(gi, carry):
        gidx = (n_groups - 1 - gi) if reverse else gi
        off = pl.multiple_of(gidx * SUBLANES, SUBLANES)
        a = a_scr[pl.ds(off, SUBLANES), :]
        b = b_scr[pl.ds(off, SUBLANES), :]
        for s in (1, 2, 4):
            if reverse:
                keep = row < SUBLANES - s
                shift = SUBLANES - s
            else:
                keep = row >= s
                shift = s
            a_nb = jnp.where(keep, pltpu.roll(a, shift, 0), 1.0)
            b_nb = jnp.where(keep, pltpu.roll(b, shift, 0), 0.0)
            b = a * b_nb + b
            a = a * a_nb
        h = a * carry + b
        a_scr[pl.ds(off, SUBLANES), :] = h
        edge = h[0:1, :] if reverse else h[SUBLANES - 1:SUBLANES, :]
        return jnp.broadcast_to(edge, (SUBLANES, D_LRU))

    hcar_ref[d] = lax.fori_loop(0, n_groups, group, hcar_ref[d])
    h_out_ref[...] = a_scr[...].astype(h_out_ref.dtype)


def _gla_tile(q_ref, k_ref, v_ref, al_ref, d, reverse, wa_ref, ba_ref, s_ref, o_out_ref):
    n = q_ref.shape[0]
    z = jnp.dot(al_ref[...], wa_ref[d], preferred_element_type=f32,
                precision=lax.Precision.HIGHEST) + ba_ref[d]
    la = _log_sigmoid(z) * (1.0 / GLA_TAU)
    la_hi = la.astype(bf16)
    la_lo = (la - la_hi.astype(f32)).astype(bf16)
    ri = lax.broadcasted_iota(i32, (n, n), 0)
    ci = lax.broadcasted_iota(i32, (n, n), 1)
    incl = (ri <= ci) if reverse else (ri >= ci)
    tri = jnp.where(incl, 1.0, 0.0).astype(bf16)
    cum = (jnp.dot(tri, la_hi, preferred_element_type=f32)
           + jnp.dot(tri, la_lo, preferred_element_type=f32))
    tot = cum[0:1, :] if reverse else cum[n - 1:n, :]

    q = q_ref[...].astype(f32)
    k = k_ref[...].astype(f32)
    v = v_ref[...]
    qa = q * (jnp.exp(cum) * (GLA_DK ** -0.5))
    kb = (k * jnp.exp(-cum)).astype(bf16)
    ke_t = (k * jnp.exp(tot - cum)).T.astype(bf16)

    state = s_ref[d]
    o_inter = jnp.dot(qa.astype(bf16), state.astype(bf16), preferred_element_type=f32)
    causal = (ri < ci) if reverse else incl
    head_of_lane = lax.broadcasted_iota(i32, (1, HK), 1) // GLA_DK
    for h in range(GLA_HEADS):
        qh = jnp.where(head_of_lane == h, qa, 0.0).astype(bf16)
        s = lax.dot_general(qh, kb, (((1,), (1,)), ((), ())), preferred_element_type=f32)
        p = jnp.where(causal, s, 0.0).astype(bf16)
        lo, hi = h * GLA_DV, (h + 1) * GLA_DV
        o_h = jnp.dot(p, v[:, lo:hi], preferred_element_type=f32) + o_inter[:, lo:hi]
        o_out_ref[:, lo:hi] = o_h.astype(o_out_ref.dtype)

    kv = jnp.dot(ke_t, v, preferred_element_type=f32)
    dec = jnp.exp(jnp.sum(la.T, axis=1, keepdims=True))
    on_diag = (lax.broadcasted_iota(i32, (HK, HV), 0) // GLA_DK
               == lax.broadcasted_iota(i32, (HK, HV), 1) // GLA_DV)
    s_ref[d] = state * dec + jnp.where(on_diag, kv, 0.0)


def _mix_body(uf_ref, qf_ref, kf_ref, vf_ref, af_ref, ub_ref, qb_ref, kb_ref, vb_ref, ab_ref,
              wg_ref, bg_ref, lam_ref, wa_ref, ba_ref,
              hf_ref, hb_ref, of_ref, ob_ref,
              a_scr, b_scr, hcar_ref, s_ref):
    @pl.when(pl.program_id(1) == 0)
    def _():
        hcar_ref[...] = jnp.zeros_like(hcar_ref)
        s_ref[...] = jnp.zeros_like(s_ref)

    _lru_tile(uf_ref, 0, False, wg_ref, bg_ref, lam_ref, a_scr, b_scr, hcar_ref, hf_ref)
    _lru_tile(ub_ref, 1, True, wg_ref, bg_ref, lam_ref, a_scr, b_scr, hcar_ref, hb_ref)
    _gla_tile(qf_ref, kf_ref, vf_ref, af_ref, 0, False, wa_ref, ba_ref, s_ref, of_ref)
    _gla_tile(qb_ref, kb_ref, vb_ref, ab_ref, 1, True, wa_ref, ba_ref, s_ref, ob_ref)


def _mix(u, q, k, v, al, wg, bg, lam, wa, ba):
    b, s, _ = u.shape
    n = min(L_MIX, s)
    nt = s // n
    fwd = lambda bi, j: (bi, j, 0)
    bwd = lambda bi, j: (bi, nt - 1 - j, 0)
    const3 = lambda bi, j: (0, 0, 0)

    def tile_specs(imap):
        return [pl.BlockSpec((None, n, c), imap) for c in (D_LRU, HK, HK, HV, 2 * GLA_RANK)]

    out_shape = [jax.ShapeDtypeStruct((b, s, c), bf16) for c in (D_LRU, D_LRU, HV, HV)]
    return pl.pallas_call(
        _mix_body,
        grid=(b, nt),
        in_specs=tile_specs(fwd) + tile_specs(bwd) + [
            pl.BlockSpec(wg.shape, const3),
            pl.BlockSpec(bg.shape, const3),
            pl.BlockSpec(lam.shape, const3),
            pl.BlockSpec(wa.shape, const3),
            pl.BlockSpec(ba.shape, const3),
        ],
        out_specs=[
            pl.BlockSpec((None, n, D_LRU), fwd),
            pl.BlockSpec((None, n, D_LRU), bwd),
            pl.BlockSpec((None, n, HV), fwd),
            pl.BlockSpec((None, n, HV), bwd),
        ],
        out_shape=out_shape,
        scratch_shapes=[
            pltpu.VMEM((n, D_LRU), f32),
            pltpu.VMEM((n, D_LRU), f32),
            pltpu.VMEM((2, SUBLANES, D_LRU), f32),
            pltpu.VMEM((2, HK, HV), f32),
        ],
        compiler_params=_cparams(2),
        name="mix",
    )(u, q, k, v, al, u, q, k, v, al, wg, bg, lam, wa, ba)


def _out_proj_body(hf_ref, hb_ref, gate_ref, of_ref, ob_ref, g2_ref, x_ref,
                   wout_ref, gn_ref, fg_ref, wr_ref, br_ref,
                   x1_ref, hn_ref, ri_ref, rg_ref, cnt_ref, cnt_scr):
    i = pl.program_id(0)
    tm = x_ref.shape[0]

    @pl.when(i == 0)
    def _():
        cnt_scr[...] = jnp.zeros_like(cnt_scr)

    hs = hf_ref[...].astype(f32) + hb_ref[...].astype(f32)
    gt = gate_ref[...].astype(f32)
    gelu = 0.5 * gt * (1.0 + jnp.tanh(0.7978845608028654 * (gt + 0.044715 * (gt * gt * gt))))
    lru_out = (hs * gelu).astype(bf16)

    o = of_ref[...].astype(f32) + ob_ref[...].astype(f32)
    g2 = g2_ref[...].astype(f32)
    silu = g2 * _sigmoid(g2)
    mix = jnp.dot(lru_out, wout_ref[0:D_LRU, :], preferred_element_type=f32)
    for h in range(GLA_HEADS):
        lo, hi = h * GLA_DV, (h + 1) * GLA_DV
        oh = o[:, lo:hi]
        on = oh * lax.rsqrt(jnp.mean(oh * oh, axis=-1, keepdims=True) + EPS)
        gla_h = (on * gn_ref[:, lo:hi] * silu[:, lo:hi]).astype(bf16)
        mix = mix + jnp.dot(gla_h, wout_ref[D_LRU + lo:D_LRU + hi, :],
                            preferred_element_type=f32)
    x1 = x_ref[...] + mix
    x1_ref[...] = x1
    hn = _rms(x1, fg_ref[...])
    hn_ref[...] = hn

    logits = jnp.dot(hn, wr_ref[...], preferred_element_type=f32,
                     precision=lax.Precision.HIGHEST) + br_ref[...]
    lane = lax.broadcasted_iota(i32, (tm, LANES), 1)
    lane_f = lane.astype(f32)
    sel_idx, sel_val = [], []
    member = jnp.zeros((tm, LANES), f32)
    for _ in range(TOP_K):
        m = jnp.max(logits, axis=-1, keepdims=True)
        idx = jnp.min(jnp.where(logits == m, lane_f, float(LANES)), axis=-1, keepdims=True)
        hit = lane_f == idx
        member = jnp.where(hit, 1.0, member)
        logits = jnp.where(hit, -jnp.inf, logits)
        sel_idx.append(idx)
        sel_val.append(m)
    ex = [jnp.exp(mv - sel_val[0]) for mv in sel_val]
    inv = 1.0 / (ex[0] + ex[1] + ex[2] + ex[3])

    ri = lax.broadcasted_iota(i32, (tm, tm), 0)
    ci = lax.broadcasted_iota(i32, (tm, tm), 1)
    before = jnp.where(ri > ci, 1.0, 0.0).astype(bf16)
    prefix = jnp.dot(before, member.astype(bf16), preferred_element_type=f32) + cnt_scr[...]
    cnt_new = cnt_scr[...] + jnp.sum(member, axis=0, keepdims=True)
    cnt_scr[...] = cnt_new
    cnt_ref[...] = cnt_new.astype(i32)

    route_i = jnp.zeros((tm, LANES), f32)
    route_g = jnp.zeros((tm, LANES), f32)
    for kk in range(TOP_K):
        rank = jnp.sum(jnp.where(lane_f == sel_idx[kk], prefix, 0.0), axis=-1, keepdims=True)
        route_i = jnp.where(lane == kk, sel_idx[kk], route_i)
        route_i = jnp.where(lane == TOP_K + kk, rank, route_i)
        route_g = jnp.where(lane == kk, ex[kk] * inv, route_g)
    ri_ref[...] = route_i.astype(i32)
    rg_ref[...] = route_g


def _out_proj(hf, hb, gate, of, ob, g2, x2, wout, gn, fg, wr, br):
    t = x2.shape[0]
    tm = min(TM_PROJ, t)
    row = lambda i: (i, 0)
    const = lambda i: (0, 0)
    out_shape = [
        jax.ShapeDtypeStruct((t, D_MODEL), f32),
        jax.ShapeDtypeStruct((t, D_MODEL), f32),
        jax.ShapeDtypeStruct((t, LANES), i32),
        jax.ShapeDtypeStruct((t, LANES), f32),
        jax.ShapeDtypeStruct((1, LANES), i32),
    ]
    return pl.pallas_call(
        _out_proj_body,
        grid=(t // tm,),
        in_specs=[
            pl.BlockSpec((tm, D_LRU), row),
            pl.BlockSpec((tm, D_LRU), row),
            pl.BlockSpec((tm, D_LRU), row),
            pl.BlockSpec((tm, HV), row),
            pl.BlockSpec((tm, HV), row),
            pl.BlockSpec((tm, HV), row),
            pl.BlockSpec((tm, D_MODEL), row),
            pl.BlockSpec((D_MODEL, D_MODEL), const),
            pl.BlockSpec((1, HV), const),
            pl.BlockSpec((1, D_MODEL), const),
            pl.BlockSpec((D_MODEL, LANES), const),
            pl.BlockSpec((1, LANES), const),
        ],
        out_specs=[
            pl.BlockSpec((tm, D_MODEL), row),
            pl.BlockSpec((tm, D_MODEL), row),
            pl.BlockSpec((tm, LANES), row),
            pl.BlockSpec((tm, LANES), row),
            pl.BlockSpec((1, LANES), const),
        ],
        out_shape=out_shape,
        scratch_shapes=[pltpu.VMEM((1, LANES), f32)],
        compiler_params=_cparams(1),
        name="out_proj",
    )(hf, hb, gate, of, ob, g2, x2, wout, gn, fg, wr, br)


def _row_copy(src, src_row, dst, dst_row, sem):
    return pltpu.make_async_copy(src.at[pl.ds(src_row, 1)], dst.at[pl.ds(dst_row, 1)], sem)


def _experts_body(be_ref, nreal_ref, idx_hbm, hn_hbm, wg_ref, wu_ref, bgt_ref, bup_ref,
                  wo_ref, bo_ref, y4_hbm, idx_smem, xbuf, ybuf, isem, gsem, ssem):
    n = pl.program_id(0)
    bm = xbuf.shape[0]
    n_real = nreal_ref[n]

    @pl.when(n_real > 0)
    def _():
        icopy = pltpu.make_async_copy(idx_hbm.at[n], idx_smem, isem)
        icopy.start()
        icopy.wait()
        for r in range(bm):
            _row_copy(hn_hbm, idx_smem[r], xbuf, r, gsem).start()
        pltpu.make_async_copy(hn_hbm.at[pl.ds(0, bm)], xbuf, gsem).wait()

        xb = xbuf[...].astype(bf16)
        gate = jnp.dot(xb, wg_ref[...], preferred_element_type=f32) + bgt_ref[...]
        up = jnp.dot(xb, wu_ref[...], preferred_element_type=f32) + bup_ref[...]
        gate = jnp.minimum(gate, SWIGLU_LIMIT)
        up = jnp.clip(up, -SWIGLU_LIMIT, SWIGLU_LIMIT)
        glu = gate * _sigmoid(SWIGLU_ALPHA * gate)
        act = ((up + 1.0) * glu).astype(bf16)
        ybuf[...] = jnp.dot(act, wo_ref[...], preferred_element_type=f32) + bo_ref[...]

        for r in range(bm):
            @pl.when(r < n_real)
            def _():
                _row_copy(ybuf, r, y4_hbm, idx_smem[bm + r], ssem).start()
        n_whole = pl.multiple_of((n_real // SUBLANES) * SUBLANES, SUBLANES)

        @pl.when(n_whole > 0)
        def _():
            pltpu.make_async_copy(ybuf.at[pl.ds(0, n_whole)], y4_hbm.at[pl.ds(0, n_whole)],
                                  ssem).wait()
        for r in range(SUBLANES - 1):
            @pl.when(n_whole + r < n_real)
            def _():
                _row_copy(ybuf, 0, y4_hbm, 0, ssem).wait()


def _experts(block_e, n_real, idx, hn, wg, wu, bgt, bup, wo, bo, y4_rows):
    n_blocks, two_bm = idx.shape
    bm = two_bm // 2
    wmap = lambda n, be, nu: (be[n], 0, 0)
    grid_spec = pltpu.PrefetchScalarGridSpec(
        num_scalar_prefetch=2,
        grid=(n_blocks,),
        in_specs=[
            pl.BlockSpec(memory_space=pl.ANY),
            pl.BlockSpec(memory_space=pl.ANY),
            pl.BlockSpec((None, D_MODEL, D_FF), wmap),
            pl.BlockSpec((None, D_MODEL, D_FF), wmap),
            pl.BlockSpec((None, 1, D_FF), wmap),
            pl.BlockSpec((None, 1, D_FF), wmap),
            pl.BlockSpec((None, D_FF, D_MODEL), wmap),
            pl.BlockSpec((None, 1, D_MODEL), wmap),
        ],
        out_specs=pl.BlockSpec(memory_space=pl.ANY),
        scratch_shapes=[
            pltpu.SMEM((two_bm,), i32),
            pltpu.VMEM((bm, D_MODEL), f32),
            pltpu.VMEM((bm, D_MODEL), f32),
            pltpu.SemaphoreType.DMA,
            pltpu.SemaphoreType.DMA,
            pltpu.SemaphoreType.DMA,
        ],
    )
    return pl.pallas_call(
        _experts_body,
        grid_spec=grid_spec,
        out_shape=jax.ShapeDtypeStruct((y4_rows, D_MODEL), f32),
        compiler_params=_cparams(1),
        name="experts",
    )(block_e, n_real, idx, hn, wg, wu, bgt, bup, wo, bo)


def _combine_body(y0_ref, y1_ref, y2_ref, y3_ref, rg_ref, x1_ref, g_ref, out_ref):
    acc = x1_ref[...]
    for kk, y_ref in enumerate((y0_ref, y1_ref, y2_ref, y3_ref)):
        acc = acc + rg_ref[:, kk:kk + 1] * y_ref[...]
    out_ref[...] = _rms(acc, g_ref[...])


def _combine(y4, rg, x1, g):
    t = x1.shape[0]
    tm = min(TM_PROJ, t)
    row = lambda i: (i, 0)
    slot_specs = [pl.BlockSpec((None, tm, D_MODEL), functools.partial(lambda kk, i: (kk, i, 0), kk))
                  for kk in range(TOP_K)]
    return pl.pallas_call(
        _combine_body,
        grid=(t // tm,),
        in_specs=slot_specs + [
            pl.BlockSpec((tm, LANES), row),
            pl.BlockSpec((tm, D_MODEL), row),
            pl.BlockSpec((1, D_MODEL), lambda i: (0, 0)),
        ],
        out_specs=pl.BlockSpec((tm, D_MODEL), row),
        out_shape=jax.ShapeDtypeStruct((t, D_MODEL), f32),
        compiler_params=_cparams(1),
        name="combine",
    )(y4, y4, y4, y4, rg, x1, g)


REPACK_ROWS = 512
MXU_COLS = 256


def _repack_body(w_ref, gate_ref, up_ref):
    ci = lax.broadcasted_iota(i32, (MXU_COLS, MXU_COLS), 0)
    ji = lax.broadcasted_iota(i32, (MXU_COLS, MXU_COLS), 1)
    src = jnp.where(ji < LANES, 2 * ji, 2 * (ji - LANES) + 1)
    perm = jnp.where(ci == src, 1.0, 0.0).astype(bf16)
    for grp in range(w_ref.shape[1] // MXU_COLS):
        blk = w_ref[:, grp * MXU_COLS:(grp + 1) * MXU_COLS].astype(bf16)
        r = jnp.dot(blk, perm, preferred_element_type=f32)
        gate_ref[:, grp * LANES:(grp + 1) * LANES] = r[:, :LANES].astype(bf16)
        up_ref[:, grp * LANES:(grp + 1) * LANES] = r[:, LANES:].astype(bf16)


def _repack_expert_in(w):
    e, d, two_f = w.shape
    imap = lambda ei, ri: (ei, ri, 0)
    out = jax.ShapeDtypeStruct((e, d, two_f // 2), bf16)
    return pl.pallas_call(
        _repack_body,
        grid=(e, d // REPACK_ROWS),
        in_specs=[pl.BlockSpec((None, REPACK_ROWS, two_f), imap)],
        out_specs=[pl.BlockSpec((None, REPACK_ROWS, two_f // 2), imap)] * 2,
        out_shape=[out, out],
        compiler_params=_cparams(2),
        name="repack",
    )(w)


def _block_diag_dense(w):
    eye = jnp.eye(LRU_BLOCKS, dtype=w.dtype)
    return jnp.einsum('hij,hg->higj', w, eye).reshape(D_LRU, D_LRU)


def _prep(mix_norm_g, w_mix_in, lru_conv_w, lru_conv_b, lru_w_r, lru_b_r, lru_w_i, lru_b_i,
          lru_lambda, gla_w_alpha, gla_b_alpha, gla_norm_g, w_mix_out, ffn_norm_g,
          w_router, b_router, w_exp_in, b_exp_in, w_exp_out, b_exp_out, final_norm_g):
    p = {}
    p["mix_g"] = mix_norm_g[0].reshape(1, D_MODEL)
    p["w_in"] = jnp.pad(w_mix_in[0], ((0, 0), (0, D_IN_PAD - w_mix_in.shape[-1]))).astype(bf16)
    p["conv_w"] = lru_conv_w[0]
    p["conv_b"] = lru_conv_b[0].reshape(1, D_LRU)
    p["wg"] = jnp.stack([
        jnp.concatenate([_block_diag_dense(lru_w_r[0, d]), _block_diag_dense(lru_w_i[0, d])], axis=1)
        for d in range(2)]).astype(bf16)
    p["bg"] = jnp.concatenate([lru_b_r[0], lru_b_i[0]], axis=-1).reshape(2, 1, 2 * D_LRU)
    p["lam"] = lru_lambda[0].reshape(2, 1, D_LRU)
    zeros = jnp.zeros((GLA_RANK, HK), f32)
    p["wa"] = jnp.stack([jnp.concatenate([gla_w_alpha[0, 0], zeros], axis=0),
                         jnp.concatenate([zeros, gla_w_alpha[0, 1]], axis=0)])
    p["ba"] = gla_b_alpha[0].reshape(2, 1, HK)
    p["gn"] = gla_norm_g[0].reshape(1, HV)
    p["w_out"] = w_mix_out[0].astype(bf16)
    p["ffn_g"] = ffn_norm_g[0].reshape(1, D_MODEL)
    p["w_r"] = jnp.pad(w_router[0], ((0, 0), (0, LANES - N_EXPERTS)))
    p["b_r"] = jnp.pad(b_router[0], (0, LANES - N_EXPERTS), constant_values=-1e30).reshape(1, LANES)
    p["w_gate"], p["w_up"] = _repack_expert_in(w_exp_in[0])
    p["b_gate"] = b_exp_in[0, :, 0::2].reshape(N_EXPERTS, 1, D_FF)
    p["b_up"] = b_exp_in[0, :, 1::2].reshape(N_EXPERTS, 1, D_FF)
    p["w_eo"] = w_exp_out[0].astype(bf16)
    p["b_eo"] = b_exp_out[0].reshape(N_EXPERTS, 1, D_MODEL)
    p["final_g"] = final_norm_g.reshape(1, D_MODEL)
    return p


def _route_tables(route_i, counts, t):
    bm = BM_EXP
    n_assign = t * TOP_K
    n_blocks = n_assign // bm + N_EXPERTS
    cap = n_blocks * bm
    cnt = counts[0, :N_EXPERTS]
    padded = ((cnt + bm - 1) // bm) * bm
    pad_ends = jnp.cumsum(padded)
    pad_starts = pad_ends - padded
    e = route_i[:, :TOP_K]
    rank = route_i[:, TOP_K:2 * TOP_K]
    pos = (pad_starts[e] + rank).reshape(-1)
    slot_of_row = jnp.full((cap,), -1, i32).at[pos].set(
        jnp.arange(n_assign, dtype=i32), unique_indices=True)
    real = slot_of_row >= 0
    src_tok = jnp.where(real, slot_of_row // TOP_K, 0)
    dst_row = jnp.where(real, (slot_of_row % TOP_K) * t + slot_of_row // TOP_K, 0)
    idx = jnp.concatenate([src_tok.reshape(n_blocks, bm), dst_row.reshape(n_blocks, bm)], axis=1)
    block_start = jnp.arange(n_blocks, dtype=i32) * bm
    block_e = jnp.minimum(jnp.sum(pad_ends[None, :] <= block_start[:, None], axis=1),
                          N_EXPERTS - 1).astype(i32)
    n_real = jnp.clip(cnt[block_e] - (block_start - pad_starts[block_e]), 0, bm).astype(i32)
    return block_e, n_real, idx


def _trunk(x, p):
    b, s, _ = x.shape
    t = b * s
    x2 = x.reshape(t, D_MODEL)
    u, gate, q, k, v, g2, al = _in_proj(x2, s, p["mix_g"], p["w_in"], p["conv_w"], p["conv_b"])
    r3 = lambda a: a.reshape(b, s, a.shape[-1])
    hf, hb, of, ob = _mix(r3(u), r3(q), r3(k), r3(v), r3(al),
                          p["wg"], p["bg"], p["lam"], p["wa"], p["ba"])
    f2 = lambda a: a.reshape(t, a.shape[-1])
    x1, hn, route_i, route_g, counts = _out_proj(
        f2(hf), f2(hb), gate, f2(of), f2(ob), g2, x2,
        p["w_out"], p["gn"], p["ffn_g"], p["w_r"], p["b_r"])
    block_e, n_real, idx = _route_tables(route_i, counts, t)
    y4 = _experts(block_e, n_real, idx, hn, p["w_gate"], p["w_up"], p["b_gate"], p["b_up"],
                  p["w_eo"], p["b_eo"], t * TOP_K)
    y = _combine(y4.reshape(TOP_K, t, D_MODEL), route_g, x1, p["final_g"])
    return y.reshape(b, s, D_MODEL)


def kernel(x_prompt, x_sample, mix_norm_g, w_mix_in, lru_conv_w, lru_conv_b, lru_w_r, lru_b_r,
           lru_w_i, lru_b_i, lru_lambda, gla_w_alpha, gla_b_alpha, gla_norm_g, w_mix_out,
           ffn_norm_g, w_router, b_router, w_exp_in, b_exp_in, w_exp_out, b_exp_out,
           final_norm_g):
    p = _prep(mix_norm_g, w_mix_in, lru_conv_w, lru_conv_b, lru_w_r, lru_b_r, lru_w_i, lru_b_i,
              lru_lambda, gla_w_alpha, gla_b_alpha, gla_norm_g, w_mix_out, ffn_norm_g,
              w_router, b_router, w_exp_in, b_exp_in, w_exp_out, b_exp_out, final_norm_g)
    return (_trunk(x_prompt, p), _trunk(x_sample, p))
```

```python
import functools

import jax
import jax.numpy as jnp
from jax import lax
from jax.experimental import pallas as pl
from jax.experimental.pallas import tpu as pltpu

f32 = jnp.float32
bf16 = jnp.bfloat16
i32 = jnp.int32

D_MODEL = 1024
D_LRU = 512
LRU_BLOCKS = 8
LRU_BW = 64
CONV_W = 4
LRU_C = 8.0
GLA_HEADS = 4
GLA_DK = 64
GLA_DV = 128
HK = GLA_HEADS * GLA_DK
HV = GLA_HEADS * GLA_DV
GLA_RANK = 16
GLA_TAU = 16.0
N_EXPERTS = 32
TOP_K = 4
D_FF = 1024
SWIGLU_LIMIT = 7.0
SWIGLU_ALPHA = 1.702
EPS = 1e-6

LANES = 128
SUBLANES = 8
D_IN_PAD = 2688
VMEM_LIMIT = 56 * 1024 * 1024

TM_PROJ = 512
L_MIX = 128
BM_EXP = 256


def _cparams(n_axes):
    return pltpu.CompilerParams(
        dimension_semantics=("arbitrary",) * n_axes, vmem_limit_bytes=VMEM_LIMIT)


def _rms(x, g):
    ms = jnp.mean(x * x, axis=-1, keepdims=True)
    return x * lax.rsqrt(ms + EPS) * g


def _sigmoid(x):
    return 1.0 / (1.0 + jnp.exp(-x))


def _log_sigmoid(x):
    return jnp.minimum(x, 0.0) - jnp.log1p(jnp.exp(-jnp.abs(x)))


def _in_proj_body(tiles_per_seq, x_ref, xprev_ref, xnext_ref, g_ref, w_ref, cw_ref, cb_ref,
                  u_ref, gate_ref, q_ref, k_ref, v_ref, g2_ref, a_ref, ext_ref):
    i = pl.program_id(0)
    tm = x_ref.shape[0]
    g = g_ref[...]
    xn = _rms(x_ref[...], g).astype(bf16)

    def proj(lo, hi):
        return jnp.dot(xn, w_ref[:, lo:hi], preferred_element_type=f32)

    gate_ref[...] = proj(512, 1024).astype(bf16)
    q_ref[...] = proj(1024, 1280).astype(bf16)
    k_ref[...] = proj(1280, 1536).astype(bf16)
    v_ref[...] = proj(1536, 2048).astype(bf16)
    g2_ref[...] = proj(2048, 2560).astype(bf16)
    a_ref[...] = proj(2560, 2688)[:, :2 * GLA_RANK]

    xh = jnp.concatenate([xprev_ref[...], xnext_ref[...]], axis=0)
    hl = jnp.dot(_rms(xh, g).astype(bf16), w_ref[:, 0:512], preferred_element_type=f32)
    pos = i % tiles_per_seq
    ext_ref[0:SUBLANES, :] = jnp.where(pos == 0, 0.0, hl[:SUBLANES])
    ext_ref[SUBLANES:SUBLANES + tm, :] = proj(0, 512)
    ext_ref[SUBLANES + tm:, :] = jnp.where(pos == tiles_per_seq - 1, 0.0, hl[SUBLANES:])
    u = cb_ref[...]
    for j in range(CONV_W):
        u = u + cw_ref[j:j + 1, :] * ext_ref[pl.ds(SUBLANES - CONV_W // 2 + j, tm), :]
    u_ref[...] = u


def _in_proj(x2, seq, g, w, cw, cb):
    t = x2.shape[0]
    tm = min(TM_PROJ, seq)
    nt = t // tm
    r = tm // SUBLANES
    last_blk = t // SUBLANES - 1
    row = lambda i: (i, 0)
    const = lambda i: (0, 0)
    out_shapes = [
        jax.ShapeDtypeStruct((t, D_LRU), f32),
        jax.ShapeDtypeStruct((t, D_LRU), bf16),
        jax.ShapeDtypeStruct((t, HK), bf16),
        jax.ShapeDtypeStruct((t, HK), bf16),
        jax.ShapeDtypeStruct((t, HV), bf16),
        jax.ShapeDtypeStruct((t, HV), bf16),
        jax.ShapeDtypeStruct((t, 2 * GLA_RANK), f32),
    ]
    return pl.pallas_call(
        functools.partial(_in_proj_body, seq // tm),
        grid=(nt,),
        in_specs=[
            pl.BlockSpec((tm, D_MODEL), row),
            pl.BlockSpec((SUBLANES, D_MODEL), lambda i: (jnp.maximum(i * r - 1, 0), 0)),
            pl.BlockSpec((SUBLANES, D_MODEL), lambda i: (jnp.minimum((i + 1) * r, last_blk), 0)),
            pl.BlockSpec((1, D_MODEL), const),
            pl.BlockSpec((D_MODEL, D_IN_PAD), const),
            pl.BlockSpec((CONV_W, D_LRU), const),
            pl.BlockSpec((1, D_LRU), const),
        ],
        out_specs=[pl.BlockSpec((tm, s.shape[1]), row) for s in out_shapes],
        out_shape=out_shapes,
        scratch_shapes=[pltpu.VMEM((tm + 2 * SUBLANES, D_LRU), f32)],
        compiler_params=_cparams(1),
        name="in_proj",
    )(x2, x2, x2, g, w, cw, cb)


def _lru_tile(u_ref, d, reverse, wg_ref, bg_ref, lam_ref, a_scr, b_scr, hcar_ref, h_out_ref):
    n = u_ref.shape[0]
    u = u_ref[...]
    gates = jnp.dot(u.astype(bf16), wg_ref[d], preferred_element_type=f32) + bg_ref[d]
    r = _sigmoid(gates[:, :D_LRU])
    ig = _sigmoid(gates[:, D_LRU:])
    log_a = r * (LRU_C * _log_sigmoid(lam_ref[d]))
    a = jnp.exp(log_a)
    a_scr[...] = a
    b_scr[...] = jnp.sqrt(1.0 - a * a) * (ig * u)

    row = lax.broadcasted_iota(i32, (SUBLANES, D_LRU), 0)
    n_groups = n // SUBLANES

    def group(gi, carry):
        gidx = (n_groups - 1 - gi) if reverse else gi
        off = pl.multiple_of(gidx * SUBLANES, SUBLANES)
        a = a_scr[pl.ds(off, SUBLANES), :]
        b = b_scr[pl.ds(off, SUBLANES), :]
        for s in (1, 2, 4):
            if reverse:
                keep = row < SUBLANES - s
                shift = SUBLANES - s
            else:
                keep = row >= s
                shift = s
            a_nb = jnp.where(keep, pltpu.roll(a, shift, 0), 1.0)
            b_nb = jnp.where(keep, pltpu.roll(b, shift, 0), 0.0)
            b = a * b_nb + b
            a = a * a_nb
        h = a * carry + b
        a_scr[pl.ds(off, SUBLANES), :] = h
        edge = h[0:1, :] if reverse else h[SUBLANES - 1:SUBLANES, :]
        return jnp.broadcast_to(edge, (SUBLANES, D_LRU))

    hcar_ref[d] = lax.fori_loop(0, n_groups, group, hcar_ref[d])
    h_out_ref[...] = a_scr[...].astype(h_out_ref.dtype)


def _gla_tile(q_ref, k_ref, v_ref, al_ref, d, reverse, wa_ref, ba_ref, s_ref, o_out_ref):
    n = q_ref.shape[0]
    z = jnp.dot(al_ref[...], wa_ref[d], preferred_element_type=f32,
                precision=lax.Precision.HIGHEST) + ba_ref[d]
    la = _log_sigmoid(z) * (1.0 / GLA_TAU)
    la_hi = la.astype(bf16)
    la_lo = (la - la_hi.astype(f32)).astype(bf16)
    ri = lax.broadcasted_iota(i32, (n, n), 0)
    ci = lax.broadcasted_iota(i32, (n, n), 1)
    incl = (ri <= ci) if reverse else (ri >= ci)
    tri = jnp.where(incl, 1.0, 0.0).astype(bf16)
    cum = (jnp.dot(tri, la_hi, preferred_element_type=f32)
           + jnp.dot(tri, la_lo, preferred_element_type=f32))
    tot = cum[0:1, :] if reverse else cum[n - 1:n, :]

    q = q_ref[...].astype(f32)
    k = k_ref[...].astype(f32)
    v = v_ref[...]
    qa = q * (jnp.exp(cum) * (GLA_DK ** -0.5))
    kb = (k * jnp.exp(-cum)).astype(bf16)
    ke_t = (k * jnp.exp(tot - cum)).T.astype(bf16)

    state = s_ref[d]
    o_inter = jnp.dot(qa.astype(bf16), state.astype(bf16), preferred_element_type=f32)
    causal = (ri < ci) if reverse else incl
    head_of_lane = lax.broadcasted_iota(i32, (1, HK), 1) // GLA_DK
    for h in range(GLA_HEADS):
        qh = jnp.where(head_of_lane == h, qa, 0.0).astype(bf16)
        s = lax.dot_general(qh, kb, (((1,), (1,)), ((), ())), preferred_element_type=f32)
        p = jnp.where(causal, s, 0.0).astype(bf16)
        lo, hi = h * GLA_DV, (h + 1) * GLA_DV
        o_h = jnp.dot(p, v[:, lo:hi], preferred_element_type=f32) + o_inter[:, lo:hi]
        o_out_ref[:, lo:hi] = o_h.astype(o_out_ref.dtype)

    kv = jnp.dot(ke_t, v, preferred_element_type=f32)
    dec = jnp.exp(jnp.sum(la.T, axis=1, keepdims=True))
    on_diag = (lax.broadcasted_iota(i32, (HK, HV), 0) // GLA_DK
               == lax.broadcasted_iota(i32, (HK, HV), 1) // GLA_DV)
    s_ref[d] = state * dec + jnp.where(on_diag, kv, 0.0)


def _mix_body(uf_ref, qf_ref, kf_ref, vf_ref, af_ref, ub_ref, qb_ref, kb_ref, vb_ref, ab_ref,
              wg_ref, bg_ref, lam_ref, wa_ref, ba_ref,
              hf_ref, hb_ref, of_ref, ob_ref,
              a_scr, b_scr, hcar_ref, s_ref):
    @pl.when(pl.program_id(1) == 0)
    def _():
        hcar_ref[...] = jnp.zeros_like(hcar_ref)
        s_ref[...] = jnp.zeros_like(s_ref)

    _lru_tile(uf_ref, 0, False, wg_ref, bg_ref, lam_ref, a_scr, b_scr, hcar_ref, hf_ref)
    _lru_tile(ub_ref, 1, True, wg_ref, bg_ref, lam_ref, a_scr, b_scr, hcar_ref, hb_ref)
    _gla_tile(qf_ref, kf_ref, vf_ref, af_ref, 0, False, wa_ref, ba_ref, s_ref, of_ref)
    _gla_tile(qb_ref, kb_ref, vb_ref, ab_ref, 1, True, wa_ref, ba_ref, s_ref, ob_ref)


def _mix(u, q, k, v, al, wg, bg, lam, wa, ba):
    b, s, _ = u.shape
    n = min(L_MIX, s)
    nt = s // n
    fwd = lambda bi, j: (bi, j, 0)
    bwd = lambda bi, j: (bi, nt - 1 - j, 0)
    const3 = lambda bi, j: (0, 0, 0)

    def tile_specs(imap):
        return [pl.BlockSpec((None, n, c), imap) for c in (D_LRU, HK, HK, HV, 2 * GLA_RANK)]

    out_shape = [jax.ShapeDtypeStruct((b, s, c), bf16) for c in (D_LRU, D_LRU, HV, HV)]
    return pl.pallas_call(
        _mix_body,
        grid=(b, nt),
        in_specs=tile_specs(fwd) + tile_specs(bwd) + [
            pl.BlockSpec(wg.shape, const3),
            pl.BlockSpec(bg.shape, const3),
            pl.BlockSpec(lam.shape, const3),
            pl.BlockSpec(wa.shape, const3),
            pl.BlockSpec(ba.shape, const3),
        ],
        out_specs=[
            pl.BlockSpec((None, n, D_LRU), fwd),
            pl.BlockSpec((None, n, D_LRU), bwd),
            pl.BlockSpec((None, n, HV), fwd),
            pl.BlockSpec((None, n, HV), bwd),
        ],
        out_shape=out_shape,
        scratch_shapes=[
            pltpu.VMEM((n, D_LRU), f32),
            pltpu.VMEM((n, D_LRU), f32),
            pltpu.VMEM((2, SUBLANES, D_LRU), f32),
            pltpu.VMEM((2, HK, HV), f32),
        ],
        compiler_params=_cparams(2),
        name="mix",
    )(u, q, k, v, al, u, q, k, v, al, wg, bg, lam, wa, ba)


def _out_proj_body(hf_ref, hb_ref, gate_ref, of_ref, ob_ref, g2_ref, x_ref,
                   wout_ref, gn_ref, fg_ref, wr_ref, br_ref,
                   x1_ref, hn_ref, ri_ref, rg_ref, cnt_ref, cnt_scr):
    i = pl.program_id(0)
    tm = x_ref.shape[0]

    @pl.when(i == 0)
    def _():
        cnt_scr[...] = jnp.zeros_like(cnt_scr)

    hs = hf_ref[...].astype(f32) + hb_ref[...].astype(f32)
    gt = gate_ref[...].astype(f32)
    gelu = 0.5 * gt * (1.0 + jnp.tanh(0.7978845608028654 * (gt + 0.044715 * (gt * gt * gt))))
    lru_out = (hs * gelu).astype(bf16)

    o = of_ref[...].astype(f32) + ob_ref[...].astype(f32)
    g2 = g2_ref[...].astype(f32)
    silu = g2 * _sigmoid(g2)
    mix = jnp.dot(lru_out, wout_ref[0:D_LRU, :], preferred_element_type=f32)
    for h in range(GLA_HEADS):
        lo, hi = h * GLA_DV, (h + 1) * GLA_DV
        oh = o[:, lo:hi]
        on = oh * lax.rsqrt(jnp.mean(oh * oh, axis=-1, keepdims=True) + EPS)
        gla_h = (on * gn_ref[:, lo:hi] * silu[:, lo:hi]).astype(bf16)
        mix = mix + jnp.dot(gla_h, wout_ref[D_LRU + lo:D_LRU + hi, :],
                            preferred_element_type=f32)
    x1 = x_ref[...] + mix
    x1_ref[...] = x1
    hn = _rms(x1, fg_ref[...])
    hn_ref[...] = hn

    logits = jnp.dot(hn, wr_ref[...], preferred_element_type=f32,
                     precision=lax.Precision.HIGHEST) + br_ref[...]
    lane = lax.broadcasted_iota(i32, (tm, LANES), 1)
    lane_f = lane.astype(f32)
    sel_idx, sel_val = [], []
    member = jnp.zeros((tm, LANES), f32)
    for _ in range(TOP_K):
        m = jnp.max(logits, axis=-1, keepdims=True)
        idx = jnp.min(jnp.where(logits == m, lane_f, float(LANES)), axis=-1, keepdims=True)
        hit = lane_f == idx
        member = jnp.where(hit, 1.0, member)
        logits = jnp.where(hit, -jnp.inf, logits)
        sel_idx.append(idx)
        sel_val.append(m)
    ex = [jnp.exp(mv - sel_val[0]) for mv in sel_val]
    inv = 1.0 / (ex[0] + ex[1] + ex[2] + ex[3])

    ri = lax.broadcasted_iota(i32, (tm, tm), 0)
    ci = lax.broadcasted_iota(i32, (tm, tm), 1)
    before = jnp.where(ri > ci, 1.0, 0.0).astype(bf16)
    prefix = jnp.dot(before, member.astype(bf16), preferred_element_type=f32) + cnt_scr[...]
    cnt_new = cnt_scr[...] + jnp.sum(member, axis=0, keepdims=True)
    cnt_scr[...] = cnt_new
    cnt_ref[...] = cnt_new.astype(i32)

    route_i = jnp.zeros((tm, LANES), f32)
    route_g = jnp.zeros((tm, LANES), f32)
    for kk in range(TOP_K):
        rank = jnp.sum(jnp.where(lane_f == sel_idx[kk], prefix, 0.0), axis=-1, keepdims=True)
        route_i = jnp.where(lane == kk, sel_idx[kk], route_i)
        route_i = jnp.where(lane == TOP_K + kk, rank, route_i)
        route_g = jnp.where(lane == kk, ex[kk] * inv, route_g)
    ri_ref[...] = route_i.astype(i32)
    rg_ref[...] = route_g


def _out_proj(hf, hb, gate, of, ob, g2, x2, wout, gn, fg, wr, br):
    t = x2.shape[0]
    tm = min(TM_PROJ, t)
    row = lambda i: (i, 0)
    const = lambda i: (0, 0)
    out_shape = [
        jax.ShapeDtypeStruct((t, D_MODEL), f32),
        jax.ShapeDtypeStruct((t, D_MODEL), f32),
        jax.ShapeDtypeStruct((t, LANES), i32),
        jax.ShapeDtypeStruct((t, LANES), f32),
        jax.ShapeDtypeStruct((1, LANES), i32),
    ]
    return pl.pallas_call(
        _out_proj_body,
        grid=(t // tm,),
        in_specs=[
            pl.BlockSpec((tm, D_LRU), row),
            pl.BlockSpec((tm, D_LRU), row),
            pl.BlockSpec((tm, D_LRU), row),
            pl.BlockSpec((tm, HV), row),
            pl.BlockSpec((tm, HV), row),
            pl.BlockSpec((tm, HV), row),
            pl.BlockSpec((tm, D_MODEL), row),
            pl.BlockSpec((D_MODEL, D_MODEL), const),
            pl.BlockSpec((1, HV), const),
            pl.BlockSpec((1, D_MODEL), const),
            pl.BlockSpec((D_MODEL, LANES), const),
            pl.BlockSpec((1, LANES), const),
        ],
        out_specs=[
            pl.BlockSpec((tm, D_MODEL), row),
            pl.BlockSpec((tm, D_MODEL), row),
            pl.BlockSpec((tm, LANES), row),
            pl.BlockSpec((tm, LANES), row),
            pl.BlockSpec((1, LANES), const),
        ],
        out_shape=out_shape,
        scratch_shapes=[pltpu.VMEM((1, LANES), f32)],
        compiler_params=_cparams(1),
        name="out_proj",
    )(hf, hb, gate, of, ob, g2, x2, wout, gn, fg, wr, br)


def _row_copy(src, src_row, dst, dst_row, sem):
    return pltpu.make_async_copy(src.at[pl.ds(src_row, 1)], dst.at[pl.ds(dst_row, 1)], sem)


def _experts_body(be_ref, nreal_ref, idx_hbm, hn_hbm, wg_ref, wu_ref, bgt_ref, bup_ref,
                  wo_ref, bo_ref, y4_hbm, idx_smem, xbuf, ybuf, isem, gsem, ssem):
    n = pl.program_id(0)
    n_blocks = pl.num_programs(0)
    bm = xbuf.shape[1]
    slot = n % 2
    other = 1 - slot

    def real_rows(m):
        return jnp.where(m < n_blocks, nreal_ref[jnp.minimum(m, n_blocks - 1)], 0)

    def idx_copy(m, s):
        return pltpu.make_async_copy(idx_hbm.at[m], idx_smem.at[s], isem.at[s])

    def start_gather(s):
        for r in range(bm):
            _row_copy(hn_hbm, idx_smem[s, r], xbuf.at[s], r, gsem.at[s]).start()

    def wait_scatter(s, rows):
        n_whole = pl.multiple_of((rows // SUBLANES) * SUBLANES, SUBLANES)

        @pl.when(n_whole > 0)
        def _():
            pltpu.make_async_copy(ybuf.at[s, pl.ds(0, n_whole)], y4_hbm.at[pl.ds(0, n_whole)],
                                  ssem.at[s]).wait()
        for r in range(SUBLANES - 1):
            @pl.when(n_whole + r < rows)
            def _():
                _row_copy(ybuf.at[s], 0, y4_hbm, 0, ssem.at[s]).wait()

    n_real = real_rows(n)
    next_real = real_rows(n + 1)

    @pl.when((n == 0) & (n_real > 0))
    def _():
        first = idx_copy(0, 0)
        first.start()
        first.wait()
        start_gather(0)

        @pl.when(next_real > 0)
        def _():
            idx_copy(1, 1).start()

    @pl.when(n_real > 0)
    def _():
        @pl.when(next_real > 0)
        def _():
            idx_copy(n + 1, other).wait()
            start_gather(other)

        pltpu.make_async_copy(hn_hbm.at[pl.ds(0, bm)], xbuf.at[slot], gsem.at[slot]).wait()

        xb = xbuf[slot].astype(bf16)
        gate = jnp.dot(xb, wg_ref[...], preferred_element_type=f32) + bgt_ref[...]
        up = jnp.dot(xb, wu_ref[...], preferred_element_type=f32) + bup_ref[...]
        gate = jnp.minimum(gate, SWIGLU_LIMIT)
        up = jnp.clip(up, -SWIGLU_LIMIT, SWIGLU_LIMIT)
        glu = gate * _sigmoid(SWIGLU_ALPHA * gate)
        act = ((up + 1.0) * glu).astype(bf16)
        ybuf[slot] = jnp.dot(act, wo_ref[...], preferred_element_type=f32) + bo_ref[...]

        for r in range(bm):
            @pl.when(r < n_real)
            def _():
                _row_copy(ybuf.at[slot], r, y4_hbm, idx_smem[slot, bm + r], ssem.at[slot]).start()

        @pl.when(n > 0)
        def _():
            wait_scatter(other, real_rows(n - 1))

        @pl.when(real_rows(n + 2) > 0)
        def _():
            idx_copy(n + 2, slot).start()

        @pl.when(next_real == 0)
        def _():
            wait_scatter(slot, n_real)


def _experts(block_e, n_real, idx, hn, wg, wu, bgt, bup, wo, bo, y4_rows):
    n_blocks, two_bm = idx.shape
    bm = two_bm // 2
    wmap = lambda n, be, nu: (be[n], 0, 0)
    grid_spec = pltpu.PrefetchScalarGridSpec(
        num_scalar_prefetch=2,
        grid=(n_blocks,),
        in_specs=[
            pl.BlockSpec(memory_space=pl.ANY),
            pl.BlockSpec(memory_space=pl.ANY),
            pl.BlockSpec((None, D_MODEL, D_FF), wmap),
            pl.BlockSpec((None, D_MODEL, D_FF), wmap),
            pl.BlockSpec((None, 1, D_FF), wmap),
            pl.BlockSpec((None, 1, D_FF), wmap),
            pl.BlockSpec((None, D_FF, D_MODEL), wmap),
            pl.BlockSpec((None, 1, D_MODEL), wmap),
        ],
        out_specs=pl.BlockSpec(memory_space=pl.ANY),
        scratch_shapes=[
            pltpu.SMEM((2, two_bm), i32),
            pltpu.VMEM((2, bm, D_MODEL), f32),
            pltpu.VMEM((2, bm, D_MODEL), f32),
            pltpu.SemaphoreType.DMA((2,)),
            pltpu.SemaphoreType.DMA((2,)),
            pltpu.SemaphoreType.DMA((2,)),
        ],
    )
    return pl.pallas_call(
        _experts_body,
        grid_spec=grid_spec,
        out_shape=jax.ShapeDtypeStruct((y4_rows, D_MODEL), f32),
        compiler_params=_cparams(1),
        name="experts",
    )(block_e, n_real, idx, hn, wg, wu, bgt, bup, wo, bo)


def _combine_body(y0_ref, y1_ref, y2_ref, y3_ref, rg_ref, x1_ref, g_ref, out_ref):
    acc = x1_ref[...]
    for kk, y_ref in enumerate((y0_ref, y1_ref, y2_ref, y3_ref)):
        acc = acc + rg_ref[:, kk:kk + 1] * y_ref[...]
    out_ref[...] = _rms(acc, g_ref[...])


def _combine(y4, rg, x1, g):
    t = x1.shape[0]
    tm = min(TM_PROJ, t)
    row = lambda i: (i, 0)
    slot_specs = [pl.BlockSpec((None, tm, D_MODEL), functools.partial(lambda kk, i: (kk, i, 0), kk))
                  for kk in range(TOP_K)]
    return pl.pallas_call(
        _combine_body,
        grid=(t // tm,),
        in_specs=slot_specs + [
            pl.BlockSpec((tm, LANES), row),
            pl.BlockSpec((tm, D_MODEL), row),
            pl.BlockSpec((1, D_MODEL), lambda i: (0, 0)),
        ],
        out_specs=pl.BlockSpec((tm, D_MODEL), row),
        out_shape=jax.ShapeDtypeStruct((t, D_MODEL), f32),
        compiler_params=_cparams(1),
        name="combine",
    )(y4, y4, y4, y4, rg, x1, g)


REPACK_ROWS = 512
MXU_COLS = 256


def _repack_body(w_ref, gate_ref, up_ref):
    ci = lax.broadcasted_iota(i32, (MXU_COLS, MXU_COLS), 0)
    ji = lax.broadcasted_iota(i32, (MXU_COLS, MXU_COLS), 1)
    src = jnp.where(ji < LANES, 2 * ji, 2 * (ji - LANES) + 1)
    perm = jnp.where(ci == src, 1.0, 0.0).astype(bf16)
    for grp in range(w_ref.shape[1] // MXU_COLS):
        blk = w_ref[:, grp * MXU_COLS:(grp + 1) * MXU_COLS].astype(bf16)
        r = jnp.dot(blk, perm, preferred_element_type=f32)
        gate_ref[:, grp * LANES:(grp + 1) * LANES] = r[:, :LANES].astype(bf16)
        up_ref[:, grp * LANES:(grp + 1) * LANES] = r[:, LANES:].astype(bf16)


def _repack_expert_in(w):
    e, d, two_f = w.shape
    imap = lambda ei, ri: (ei, ri, 0)
    out = jax.ShapeDtypeStruct((e, d, two_f // 2), bf16)
    return pl.pallas_call(
        _repack_body,
        grid=(e, d // REPACK_ROWS),
        in_specs=[pl.BlockSpec((None, REPACK_ROWS, two_f), imap)],
        out_specs=[pl.BlockSpec((None, REPACK_ROWS, two_f // 2), imap)] * 2,
        out_shape=[out, out],
        compiler_params=_cparams(2),
        name="repack",
    )(w)


def _block_diag_dense(w):
    eye = jnp.eye(LRU_BLOCKS, dtype=w.dtype)
    return jnp.einsum('hij,hg->higj', w, eye).reshape(D_LRU, D_LRU)


def _prep(mix_norm_g, w_mix_in, lru_conv_w, lru_conv_b, lru_w_r, lru_b_r, lru_w_i, lru_b_i,
          lru_lambda, gla_w_alpha, gla_b_alpha, gla_norm_g, w_mix_out, ffn_norm_g,
          w_router, b_router, w_exp_in, b_exp_in, w_exp_out, b_exp_out, final_norm_g):
    p = {}
    p["mix_g"] = mix_norm_g[0].reshape(1, D_MODEL)
    p["w_in"] = jnp.pad(w_mix_in[0], ((0, 0), (0, D_IN_PAD - w_mix_in.shape[-1]))).astype(bf16)
    p["conv_w"] = lru_conv_w[0]
    p["conv_b"] = lru_conv_b[0].reshape(1, D_LRU)
    p["wg"] = jnp.stack([
        jnp.concatenate([_block_diag_dense(lru_w_r[0, d]), _block_diag_dense(lru_w_i[0, d])], axis=1)
        for d in range(2)]).astype(bf16)
    p["bg"] = jnp.concatenate([lru_b_r[0], lru_b_i[0]], axis=-1).reshape(2, 1, 2 * D_LRU)
    p["lam"] = lru_lambda[0].reshape(2, 1, D_LRU)
    zeros = jnp.zeros((GLA_RANK, HK), f32)
    p["wa"] = jnp.stack([jnp.concatenate([gla_w_alpha[0, 0], zeros], axis=0),
                         jnp.concatenate([zeros, gla_w_alpha[0, 1]], axis=0)])
    p["ba"] = gla_b_alpha[0].reshape(2, 1, HK)
    p["gn"] = gla_norm_g[0].reshape(1, HV)
    p["w_out"] = w_mix_out[0].astype(bf16)
    p["ffn_g"] = ffn_norm_g[0].reshape(1, D_MODEL)
    p["w_r"] = jnp.pad(w_router[0], ((0, 0), (0, LANES - N_EXPERTS)))
    p["b_r"] = jnp.pad(b_router[0], (0, LANES - N_EXPERTS), constant_values=-1e30).reshape(1, LANES)
    p["w_gate"], p["w_up"] = _repack_expert_in(w_exp_in[0])
    p["b_gate"] = b_exp_in[0, :, 0::2].reshape(N_EXPERTS, 1, D_FF)
    p["b_up"] = b_exp_in[0, :, 1::2].reshape(N_EXPERTS, 1, D_FF)
    p["w_eo"] = w_exp_out[0].astype(bf16)
    p["b_eo"] = b_exp_out[0].reshape(N_EXPERTS, 1, D_MODEL)
    p["final_g"] = final_norm_g.reshape(1, D_MODEL)
    return p


def _route_tables(route_i, counts, t):
    bm = BM_EXP
    n_assign = t * TOP_K
    n_blocks = n_assign // bm + N_EXPERTS
    cap = n_blocks * bm
    cnt = counts[0, :N_EXPERTS]
    padded = ((cnt + bm - 1) // bm) * bm
    pad_ends = jnp.cumsum(padded)
    pad_starts = pad_ends - padded
    e = route_i[:, :TOP_K]
    rank = route_i[:, TOP_K:2 * TOP_K]
    pos = (pad_starts[e] + rank).reshape(-1)
    slot_of_row = jnp.full((cap,), -1, i32).at[pos].set(
        jnp.arange(n_assign, dtype=i32), unique_indices=True)
    real = slot_of_row >= 0
    src_tok = jnp.where(real, slot_of_row // TOP_K, 0)
    dst_row = jnp.where(real, (slot_of_row % TOP_K) * t + slot_of_row // TOP_K, 0)
    idx = jnp.concatenate([src_tok.reshape(n_blocks, bm), dst_row.reshape(n_blocks, bm)], axis=1)
    block_start = jnp.arange(n_blocks, dtype=i32) * bm
    block_e = jnp.minimum(jnp.sum(pad_ends[None, :] <= block_start[:, None], axis=1),
                          N_EXPERTS - 1).astype(i32)
    n_real = jnp.clip(cnt[block_e] - (block_start - pad_starts[block_e]), 0, bm).astype(i32)
    return block_e, n_real, idx


def _trunk(x, p):
    b, s, _ = x.shape
    t = b * s
    x2 = x.reshape(t, D_MODEL)
    u, gate, q, k, v, g2, al = _in_proj(x2, s, p["mix_g"], p["w_in"], p["conv_w"], p["conv_b"])
    r3 = lambda a: a.reshape(b, s, a.shape[-1])
    hf, hb, of, ob = _mix(r3(u), r3(q), r3(k), r3(v), r3(al),
                          p["wg"], p["bg"], p["lam"], p["wa"], p["ba"])
    f2 = lambda a: a.reshape(t, a.shape[-1])
    x1, hn, route_i, route_g, counts = _out_proj(
        f2(hf), f2(hb), gate, f2(of), f2(ob), g2, x2,
        p["w_out"], p["gn"], p["ffn_g"], p["w_r"], p["b_r"])
    block_e, n_real, idx = _route_tables(route_i, counts, t)
    y4 = _experts(block_e, n_real, idx, hn, p["w_gate"], p["w_up"], p["b_gate"], p["b_up"],
                  p["w_eo"], p["b_eo"], t * TOP_K)
    y = _combine(y4.reshape(TOP_K, t, D_MODEL), route_g, x1, p["final_g"])
    return y.reshape(b, s, D_MODEL)


def kernel(x_prompt, x_sample, mix_norm_g, w_mix_in, lru_conv_w, lru_conv_b, lru_w_r, lru_b_r,
           lru_w_i, lru_b_i, lru_lambda, gla_w_alpha, gla_b_alpha, gla_norm_g, w_mix_out,
           ffn_norm_g, w_router, b_router, w_exp_in, b_exp_in, w_exp_out, b_exp_out,
           final_norm_g):
    p = _prep(mix_norm_g, w_mix_in, lru_conv_w, lru_conv_b, lru_w_r, lru_b_r, lru_w_i, lru_b_i,
              lru_lambda, gla_w_alpha, gla_b_alpha, gla_norm_g, w_mix_out, ffn_norm_g,
              w_router, b_router, w_exp_in, b_exp_in, w_exp_out, b_exp_out, final_norm_g)
    return (_trunk(x_prompt, p), _trunk(x_sample, p))
```

```python
import functools

import jax
import jax.numpy as jnp
from jax import lax
from jax.experimental import pallas as pl
from jax.experimental.pallas import tpu as pltpu

f32 = jnp.float32
bf16 = jnp.bfloat16
i32 = jnp.int32

D_MODEL = 1024
D_LRU = 512
LRU_BLOCKS = 8
LRU_BW = 64
CONV_W = 4
LRU_C = 8.0
GLA_HEADS = 4
GLA_DK = 64
GLA_DV = 128
HK = GLA_HEADS * GLA_DK
HV = GLA_HEADS * GLA_DV
GLA_RANK = 16
GLA_TAU = 16.0
N_EXPERTS = 32
TOP_K = 4
D_FF = 1024
SWIGLU_LIMIT = 7.0
SWIGLU_ALPHA = 1.702
EPS = 1e-6

LANES = 128
SUBLANES = 8
D_IN_PAD = 2688
VMEM_LIMIT = 56 * 1024 * 1024

TM_PROJ = 512
L_MIX = 128
BM_EXP = 256


def _cparams(n_axes):
    return pltpu.CompilerParams(
        dimension_semantics=("arbitrary",) * n_axes, vmem_limit_bytes=VMEM_LIMIT)


def _rms(x, g):
    ms = jnp.mean(x * x, axis=-1, keepdims=True)
    return x * lax.rsqrt(ms + EPS) * g


def _sigmoid(x):
    return 1.0 / (1.0 + jnp.exp(-x))


SLAB = D_MODEL // LANES


def _store_slabs(ref, x):
    n = x.shape[0]
    for s in range(SLAB):
        ref[pl.ds(s, n, stride=SLAB), :] = x[:, s * LANES:(s + 1) * LANES]


def _load_slabs(ref):
    n = ref.shape[0] // SLAB
    return jnp.concatenate([ref[pl.ds(s, n, stride=SLAB), :] for s in range(SLAB)], axis=1)


def _log_sigmoid(x):
    return jnp.minimum(x, 0.0) - jnp.log1p(jnp.exp(-jnp.abs(x)))


def _in_proj_body(tiles_per_seq, x_ref, xprev_ref, xnext_ref, g_ref, w_ref, cw_ref, cb_ref,
                  u_ref, gate_ref, q_ref, k_ref, v_ref, g2_ref, a_ref, ext_ref):
    i = pl.program_id(0)
    tm = x_ref.shape[0]
    g = g_ref[...]
    xn = _rms(x_ref[...], g).astype(bf16)

    def proj(lo, hi):
        return jnp.dot(xn, w_ref[:, lo:hi], preferred_element_type=f32)

    gate_ref[...] = proj(512, 1024).astype(bf16)
    q_ref[...] = proj(1024, 1280).astype(bf16)
    k_ref[...] = proj(1280, 1536).astype(bf16)
    v_ref[...] = proj(1536, 2048).astype(bf16)
    g2_ref[...] = proj(2048, 2560).astype(bf16)
    a_ref[...] = proj(2560, 2688)[:, :2 * GLA_RANK]

    xh = jnp.concatenate([xprev_ref[...], xnext_ref[...]], axis=0)
    hl = jnp.dot(_rms(xh, g).astype(bf16), w_ref[:, 0:512], preferred_element_type=f32)
    pos = i % tiles_per_seq
    ext_ref[0:SUBLANES, :] = jnp.where(pos == 0, 0.0, hl[:SUBLANES])
    ext_ref[SUBLANES:SUBLANES + tm, :] = proj(0, 512)
    ext_ref[SUBLANES + tm:, :] = jnp.where(pos == tiles_per_seq - 1, 0.0, hl[SUBLANES:])
    u = cb_ref[...]
    for j in range(CONV_W):
        u = u + cw_ref[j:j + 1, :] * ext_ref[pl.ds(SUBLANES - CONV_W // 2 + j, tm), :]
    u_ref[...] = u


def _in_proj(x2, seq, g, w, cw, cb):
    t = x2.shape[0]
    tm = min(TM_PROJ, seq)
    nt = t // tm
    r = tm // SUBLANES
    last_blk = t // SUBLANES - 1
    row = lambda i: (i, 0)
    const = lambda i: (0, 0)
    out_shapes = [
        jax.ShapeDtypeStruct((t, D_LRU), f32),
        jax.ShapeDtypeStruct((t, D_LRU), bf16),
        jax.ShapeDtypeStruct((t, HK), bf16),
        jax.ShapeDtypeStruct((t, HK), bf16),
        jax.ShapeDtypeStruct((t, HV), bf16),
        jax.ShapeDtypeStruct((t, HV), bf16),
        jax.ShapeDtypeStruct((t, 2 * GLA_RANK), f32),
    ]
    return pl.pallas_call(
        functools.partial(_in_proj_body, seq // tm),
        grid=(nt,),
        in_specs=[
            pl.BlockSpec((tm, D_MODEL), row),
            pl.BlockSpec((SUBLANES, D_MODEL), lambda i: (jnp.maximum(i * r - 1, 0), 0)),
            pl.BlockSpec((SUBLANES, D_MODEL), lambda i: (jnp.minimum((i + 1) * r, last_blk), 0)),
            pl.BlockSpec((1, D_MODEL), const),
            pl.BlockSpec((D_MODEL, D_IN_PAD), const),
            pl.BlockSpec((CONV_W, D_LRU), const),
            pl.BlockSpec((1, D_LRU), const),
        ],
        out_specs=[pl.BlockSpec((tm, s.shape[1]), row) for s in out_shapes],
        out_shape=out_shapes,
        scratch_shapes=[pltpu.VMEM((tm + 2 * SUBLANES, D_LRU), f32)],
        compiler_params=_cparams(1),
        name="in_proj",
    )(x2, x2, x2, g, w, cw, cb)


def _lru_tile(u_ref, d, reverse, wg_ref, bg_ref, lam_ref, a_scr, b_scr, hcar_ref, h_out_ref):
    n = u_ref.shape[0]
    u = u_ref[...]
    gates = jnp.dot(u.astype(bf16), wg_ref[d], preferred_element_type=f32) + bg_ref[d]
    r = _sigmoid(gates[:, :D_LRU])
    ig = _sigmoid(gates[:, D_LRU:])
    log_a = r * (LRU_C * _log_sigmoid(lam_ref[d]))
    a = jnp.exp(log_a)
    a_scr[...] = a
    b_scr[...] = jnp.sqrt(1.0 - a * a) * (ig * u)

    row = lax.broadcasted_iota(i32, (SUBLANES, D_LRU), 0)
    n_groups = n // SUBLANES

    def group(gi, carry):
        gidx = (n_groups - 1 - gi) if reverse else gi
        off = pl.multiple_of(gidx * SUBLANES, SUBLANES)
        a = a_scr[pl.ds(off, SUBLANES), :]
        b = b_scr[pl.ds(off, SUBLANES), :]
        for s in (1, 2, 4):
            if reverse:
                keep = row < SUBLANES - s
                shift = SUBLANES - s
            else:
                keep = row >= s
                shift = s
            a_nb = jnp.where(keep, pltpu.roll(a, shift, 0), 1.0)
            b_nb = jnp.where(keep, pltpu.roll(b, shift, 0), 0.0)
            b = a * b_nb + b
            a = a * a_nb
        h = a * carry + b
        a_scr[pl.ds(off, SUBLANES), :] = h
        edge = h[0:1, :] if reverse else h[SUBLANES - 1:SUBLANES, :]
        return jnp.broadcast_to(edge, (SUBLANES, D_LRU))

    hcar_ref[d] = lax.fori_loop(0, n_groups, group, hcar_ref[d])
    h_out_ref[...] = a_scr[...].astype(h_out_ref.dtype)


def _gla_tile(q_ref, k_ref, v_ref, al_ref, d, reverse, wa_ref, ba_ref, s_ref, o_out_ref):
    n = q_ref.shape[0]
    z = jnp.dot(al_ref[...], wa_ref[d], preferred_element_type=f32,
                precision=lax.Precision.HIGHEST) + ba_ref[d]
    la = _log_sigmoid(z) * (1.0 / GLA_TAU)
    la_hi = la.astype(bf16)
    la_lo = (la - la_hi.astype(f32)).astype(bf16)
    ri = lax.broadcasted_iota(i32, (n, n), 0)
    ci = lax.broadcasted_iota(i32, (n, n), 1)
    incl = (ri <= ci) if reverse else (ri >= ci)
    tri = jnp.where(incl, 1.0, 0.0).astype(bf16)
    cum = (jnp.dot(tri, la_hi, preferred_element_type=f32)
           + jnp.dot(tri, la_lo, preferred_element_type=f32))
    tot = cum[0:1, :] if reverse else cum[n - 1:n, :]

    q = q_ref[...].astype(f32)
    k = k_ref[...].astype(f32)
    v = v_ref[...]
    qa = q * (jnp.exp(cum) * (GLA_DK ** -0.5))
    kb = (k * jnp.exp(-cum)).astype(bf16)
    ke_t = (k * jnp.exp(tot - cum)).T.astype(bf16)

    state = s_ref[d]
    o_inter = jnp.dot(qa.astype(bf16), state.astype(bf16), preferred_element_type=f32)
    causal = (ri < ci) if reverse else incl
    head_of_lane = lax.broadcasted_iota(i32, (1, HK), 1) // GLA_DK
    for h in range(GLA_HEADS):
        qh = jnp.where(head_of_lane == h, qa, 0.0).astype(bf16)
        s = lax.dot_general(qh, kb, (((1,), (1,)), ((), ())), preferred_element_type=f32)
        p = jnp.where(causal, s, 0.0).astype(bf16)
        lo, hi = h * GLA_DV, (h + 1) * GLA_DV
        o_h = jnp.dot(p, v[:, lo:hi], preferred_element_type=f32) + o_inter[:, lo:hi]
        o_out_ref[:, lo:hi] = o_h.astype(o_out_ref.dtype)

    kv = jnp.dot(ke_t, v, preferred_element_type=f32)
    dec = jnp.exp(jnp.sum(la.T, axis=1, keepdims=True))
    on_diag = (lax.broadcasted_iota(i32, (HK, HV), 0) // GLA_DK
               == lax.broadcasted_iota(i32, (HK, HV), 1) // GLA_DV)
    s_ref[d] = state * dec + jnp.where(on_diag, kv, 0.0)


def _mix_body(uf_ref, qf_ref, kf_ref, vf_ref, af_ref, ub_ref, qb_ref, kb_ref, vb_ref, ab_ref,
              wg_ref, bg_ref, lam_ref, wa_ref, ba_ref,
              hf_ref, hb_ref, of_ref, ob_ref,
              a_scr, b_scr, hcar_ref, s_ref):
    @pl.when(pl.program_id(1) == 0)
    def _():
        hcar_ref[...] = jnp.zeros_like(hcar_ref)
        s_ref[...] = jnp.zeros_like(s_ref)

    _lru_tile(uf_ref, 0, False, wg_ref, bg_ref, lam_ref, a_scr, b_scr, hcar_ref, hf_ref)
    _lru_tile(ub_ref, 1, True, wg_ref, bg_ref, lam_ref, a_scr, b_scr, hcar_ref, hb_ref)
    _gla_tile(qf_ref, kf_ref, vf_ref, af_ref, 0, False, wa_ref, ba_ref, s_ref, of_ref)
    _gla_tile(qb_ref, kb_ref, vb_ref, ab_ref, 1, True, wa_ref, ba_ref, s_ref, ob_ref)


def _mix(u, q, k, v, al, wg, bg, lam, wa, ba):
    b, s, _ = u.shape
    n = min(L_MIX, s)
    nt = s // n
    fwd = lambda bi, j: (bi, j, 0)
    bwd = lambda bi, j: (bi, nt - 1 - j, 0)
    const3 = lambda bi, j: (0, 0, 0)

    def tile_specs(imap):
        return [pl.BlockSpec((None, n, c), imap) for c in (D_LRU, HK, HK, HV, 2 * GLA_RANK)]

    out_shape = [jax.ShapeDtypeStruct((b, s, c), bf16) for c in (D_LRU, D_LRU, HV, HV)]
    return pl.pallas_call(
        _mix_body,
        grid=(b, nt),
        in_specs=tile_specs(fwd) + tile_specs(bwd) + [
            pl.BlockSpec(wg.shape, const3),
            pl.BlockSpec(bg.shape, const3),
            pl.BlockSpec(lam.shape, const3),
            pl.BlockSpec(wa.shape, const3),
            pl.BlockSpec(ba.shape, const3),
        ],
        out_specs=[
            pl.BlockSpec((None, n, D_LRU), fwd),
            pl.BlockSpec((None, n, D_LRU), bwd),
            pl.BlockSpec((None, n, HV), fwd),
            pl.BlockSpec((None, n, HV), bwd),
        ],
        out_shape=out_shape,
        scratch_shapes=[
            pltpu.VMEM((n, D_LRU), f32),
            pltpu.VMEM((n, D_LRU), f32),
            pltpu.VMEM((2, SUBLANES, D_LRU), f32),
            pltpu.VMEM((2, HK, HV), f32),
        ],
        compiler_params=_cparams(2),
        name="mix",
    )(u, q, k, v, al, u, q, k, v, al, wg, bg, lam, wa, ba)


def _out_proj_body(hf_ref, hb_ref, gate_ref, of_ref, ob_ref, g2_ref, x_ref,
                   wout_ref, gn_ref, fg_ref, wr_ref, br_ref,
                   x1_ref, hn_ref, ri_ref, rg_ref, cnt_ref, cnt_scr):
    i = pl.program_id(0)
    tm = x_ref.shape[0]

    @pl.when(i == 0)
    def _():
        cnt_scr[...] = jnp.zeros_like(cnt_scr)

    hs = hf_ref[...].astype(f32) + hb_ref[...].astype(f32)
    gt = gate_ref[...].astype(f32)
    gelu = 0.5 * gt * (1.0 + jnp.tanh(0.7978845608028654 * (gt + 0.044715 * (gt * gt * gt))))
    lru_out = (hs * gelu).astype(bf16)

    o = of_ref[...].astype(f32) + ob_ref[...].astype(f32)
    g2 = g2_ref[...].astype(f32)
    silu = g2 * _sigmoid(g2)
    mix = jnp.dot(lru_out, wout_ref[0:D_LRU, :], preferred_element_type=f32)
    for h in range(GLA_HEADS):
        lo, hi = h * GLA_DV, (h + 1) * GLA_DV
        oh = o[:, lo:hi]
        on = oh * lax.rsqrt(jnp.mean(oh * oh, axis=-1, keepdims=True) + EPS)
        gla_h = (on * gn_ref[:, lo:hi] * silu[:, lo:hi]).astype(bf16)
        mix = mix + jnp.dot(gla_h, wout_ref[D_LRU + lo:D_LRU + hi, :],
                            preferred_element_type=f32)
    x1 = x_ref[...] + mix
    x1_ref[...] = x1
    hn = _rms(x1, fg_ref[...])
    _store_slabs(hn_ref, hn)

    logits = jnp.dot(hn, wr_ref[...], preferred_element_type=f32,
                     precision=lax.Precision.HIGHEST) + br_ref[...]
    lane = lax.broadcasted_iota(i32, (tm, LANES), 1)
    lane_f = lane.astype(f32)
    sel_idx, sel_val = [], []
    member = jnp.zeros((tm, LANES), f32)
    for _ in range(TOP_K):
        m = jnp.max(logits, axis=-1, keepdims=True)
        idx = jnp.min(jnp.where(logits == m, lane_f, float(LANES)), axis=-1, keepdims=True)
        hit = lane_f == idx
        member = jnp.where(hit, 1.0, member)
        logits = jnp.where(hit, -jnp.inf, logits)
        sel_idx.append(idx)
        sel_val.append(m)
    ex = [jnp.exp(mv - sel_val[0]) for mv in sel_val]
    inv = 1.0 / (ex[0] + ex[1] + ex[2] + ex[3])

    ri = lax.broadcasted_iota(i32, (tm, tm), 0)
    ci = lax.broadcasted_iota(i32, (tm, tm), 1)
    before = jnp.where(ri > ci, 1.0, 0.0).astype(bf16)
    prefix = jnp.dot(before, member.astype(bf16), preferred_element_type=f32) + cnt_scr[...]
    cnt_new = cnt_scr[...] + jnp.sum(member, axis=0, keepdims=True)
    cnt_scr[...] = cnt_new
    cnt_ref[...] = cnt_new.astype(i32)

    route_i = jnp.zeros((tm, LANES), f32)
    route_g = jnp.zeros((tm, LANES), f32)
    for kk in range(TOP_K):
        rank = jnp.sum(jnp.where(lane_f == sel_idx[kk], prefix, 0.0), axis=-1, keepdims=True)
        route_i = jnp.where(lane == kk, sel_idx[kk], route_i)
        route_i = jnp.where(lane == TOP_K + kk, rank, route_i)
        route_g = jnp.where(lane == kk, ex[kk] * inv, route_g)
    ri_ref[...] = route_i.astype(i32)
    rg_ref[...] = route_g


def _out_proj(hf, hb, gate, of, ob, g2, x2, wout, gn, fg, wr, br):
    t = x2.shape[0]
    tm = min(TM_PROJ, t)
    row = lambda i: (i, 0)
    const = lambda i: (0, 0)
    out_shape = [
        jax.ShapeDtypeStruct((t, D_MODEL), f32),
        jax.ShapeDtypeStruct((t * SLAB, LANES), f32),
        jax.ShapeDtypeStruct((t, LANES), i32),
        jax.ShapeDtypeStruct((t, LANES), f32),
        jax.ShapeDtypeStruct((1, LANES), i32),
    ]
    return pl.pallas_call(
        _out_proj_body,
        grid=(t // tm,),
        in_specs=[
            pl.BlockSpec((tm, D_LRU), row),
            pl.BlockSpec((tm, D_LRU), row),
            pl.BlockSpec((tm, D_LRU), row),
            pl.BlockSpec((tm, HV), row),
            pl.BlockSpec((tm, HV), row),
            pl.BlockSpec((tm, HV), row),
            pl.BlockSpec((tm, D_MODEL), row),
            pl.BlockSpec((D_MODEL, D_MODEL), const),
            pl.BlockSpec((1, HV), const),
            pl.BlockSpec((1, D_MODEL), const),
            pl.BlockSpec((D_MODEL, LANES), const),
            pl.BlockSpec((1, LANES), const),
        ],
        out_specs=[
            pl.BlockSpec((tm, D_MODEL), row),
            pl.BlockSpec((tm * SLAB, LANES), row),
            pl.BlockSpec((tm, LANES), row),
            pl.BlockSpec((tm, LANES), row),
            pl.BlockSpec((1, LANES), const),
        ],
        out_shape=out_shape,
        scratch_shapes=[pltpu.VMEM((1, LANES), f32)],
        compiler_params=_cparams(1),
        name="out_proj",
    )(hf, hb, gate, of, ob, g2, x2, wout, gn, fg, wr, br)


def _slab_copy(src, src_sub, dst, dst_sub, sem):
    return pltpu.make_async_copy(src.at[pl.ds(src_sub, SLAB)], dst.at[pl.ds(dst_sub, SLAB)], sem)


def _experts_body(be_ref, nreal_ref, idx_hbm, hn_hbm, wg_ref, wu_ref, bgt_ref, bup_ref,
                  wo_ref, bo_ref, y4_hbm, idx_smem, xbuf0, xbuf1, ybuf0, ybuf1, isem, gsem, ssem):
    n = pl.program_id(0)
    n_blocks = pl.num_programs(0)
    bm = xbuf0.shape[0] // SLAB
    spare_sub = y4_hbm.shape[0] - bm * SLAB

    def used(m):
        inside = (m >= 0) & (m < n_blocks)
        return jnp.where(inside, nreal_ref[jnp.clip(m, 0, n_blocks - 1)], 0) > 0

    def idx_copy(m):
        ring = m % IDX_RING
        return pltpu.make_async_copy(idx_hbm.at[jnp.minimum(m, n_blocks - 1)], idx_smem.at[ring],
                                     isem.at[ring])

    def issue_gather(m, xb, sem):
        ring = m % IDX_RING
        for r in range(bm):
            src = pl.multiple_of(idx_smem[ring, r], SLAB)
            _slab_copy(hn_hbm, src, xb, r * SLAB, sem).start()

    def wait_gather(xb, sem):
        pltpu.make_async_copy(hn_hbm.at[pl.ds(0, bm * SLAB)], xb, sem).wait()

    def issue_scatter(m, yb):
        ring = m % IDX_RING
        for r in range(bm):
            dst = pl.multiple_of(idx_smem[ring, bm + r], SLAB)
            _slab_copy(yb, r * SLAB, y4_hbm, dst, ssem).start()

    def wait_scatter(yb):
        pltpu.make_async_copy(yb, y4_hbm.at[pl.ds(0, bm * SLAB)], ssem).wait()

    def compute(xb, yb):
        x = _load_slabs(xb).astype(bf16)
        gate = jnp.dot(x, wg_ref[...], preferred_element_type=f32) + bgt_ref[...]
        up = jnp.dot(x, wu_ref[...], preferred_element_type=f32) + bup_ref[...]
        gate = jnp.minimum(gate, SWIGLU_LIMIT)
        up = jnp.clip(up, -SWIGLU_LIMIT, SWIGLU_LIMIT)
        glu = gate * _sigmoid(SWIGLU_ALPHA * gate)
        act = ((up + 1.0) * glu).astype(bf16)
        _store_slabs(yb, jnp.dot(act, wo_ref[...], preferred_element_type=f32) + bo_ref[...])

    bufs = ((xbuf0, ybuf0, gsem.at[0], xbuf1, ybuf1, gsem.at[1]),
            (xbuf1, ybuf1, gsem.at[1], xbuf0, ybuf0, gsem.at[0]))

    def step(parity, has_prev):
        x_cur, y_cur, g_cur, x_alt, y_alt, g_alt = bufs[parity]
        idx_copy(n + 1).wait()
        issue_gather(n + 1, x_alt, g_alt)
        if has_prev:
            issue_scatter(n - 1, y_alt)
        idx_copy(n + 2).start()
        wait_gather(x_cur, g_cur)
        compute(x_cur, y_cur)
        if has_prev:
            wait_scatter(y_alt)

    def drain(parity):
        x_cur, _, g_cur, _, y_alt, _ = bufs[parity]
        idx_copy(n + 1).wait()
        wait_gather(x_cur, g_cur)
        issue_scatter(n - 1, y_alt)
        wait_scatter(y_alt)
        fill = pltpu.make_async_copy(y_alt, y4_hbm.at[pl.ds(spare_sub, bm * SLAB)], ssem)
        fill.start()
        fill.wait()

    @pl.when((n == 0) & used(0))
    def _():
        first = idx_copy(0)
        first.start()
        first.wait()
        idx_copy(1).start()
        issue_gather(0, xbuf0, gsem.at[0])
        step(0, False)

    for parity in range(2):
        @pl.when((n > 0) & (n % 2 == parity) & used(n))
        def _():
            step(parity, True)

        @pl.when((n % 2 == parity) & used(n - 1) & jnp.logical_not(used(n)))
        def _():
            drain(parity)


IDX_RING = 4


def _experts(block_e, n_real, idx, hn, wg, wu, bgt, bup, wo, bo, n_out):
    n_blocks, two_bm = idx.shape
    bm = two_bm // 2
    wmap = lambda n, be, nu: (be[n], 0, 0)
    grid_spec = pltpu.PrefetchScalarGridSpec(
        num_scalar_prefetch=2,
        grid=(n_blocks,),
        in_specs=[
            pl.BlockSpec(memory_space=pl.ANY),
            pl.BlockSpec(memory_space=pl.ANY),
            pl.BlockSpec((None, D_MODEL, D_FF), wmap),
            pl.BlockSpec((None, D_MODEL, D_FF), wmap),
            pl.BlockSpec((None, 1, D_FF), wmap),
            pl.BlockSpec((None, 1, D_FF), wmap),
            pl.BlockSpec((None, D_FF, D_MODEL), wmap),
            pl.BlockSpec((None, 1, D_MODEL), wmap),
        ],
        out_specs=pl.BlockSpec(memory_space=pl.ANY),
        scratch_shapes=[
            pltpu.SMEM((IDX_RING, two_bm), i32),
            pltpu.VMEM((bm * SLAB, LANES), f32),
            pltpu.VMEM((bm * SLAB, LANES), f32),
            pltpu.VMEM((bm * SLAB, LANES), f32),
            pltpu.VMEM((bm * SLAB, LANES), f32),
            pltpu.SemaphoreType.DMA((IDX_RING,)),
            pltpu.SemaphoreType.DMA((2,)),
            pltpu.SemaphoreType.DMA,
        ],
    )
    return pl.pallas_call(
        _experts_body,
        grid_spec=grid_spec,
        out_shape=jax.ShapeDtypeStruct(((n_out + bm) * SLAB, LANES), f32),
        compiler_params=_cparams(1),
        name="experts",
    )(block_e, n_real, idx, hn, wg, wu, bgt, bup, wo, bo)


def _combine_body(y0_ref, y1_ref, y2_ref, y3_ref, rg_ref, x1_ref, g_ref, out_ref):
    acc = x1_ref[...]
    for kk, y_ref in enumerate((y0_ref, y1_ref, y2_ref, y3_ref)):
        acc = acc + rg_ref[:, kk:kk + 1] * _load_slabs(y_ref)
    out_ref[...] = _rms(acc, g_ref[...])


def _combine(y4, rg, x1, g):
    t = x1.shape[0]
    tm = min(TM_PROJ, t)
    row = lambda i: (i, 0)
    slot_specs = [pl.BlockSpec((tm * SLAB, LANES),
                               functools.partial(lambda kk, i: (kk * (t // tm) + i, 0), kk))
                  for kk in range(TOP_K)]
    return pl.pallas_call(
        _combine_body,
        grid=(t // tm,),
        in_specs=slot_specs + [
            pl.BlockSpec((tm, LANES), row),
            pl.BlockSpec((tm, D_MODEL), row),
            pl.BlockSpec((1, D_MODEL), lambda i: (0, 0)),
        ],
        out_specs=pl.BlockSpec((tm, D_MODEL), row),
        out_shape=jax.ShapeDtypeStruct((t, D_MODEL), f32),
        compiler_params=_cparams(1),
        name="combine",
    )(y4, y4, y4, y4, rg, x1, g)


REPACK_ROWS = 512
MXU_COLS = 256


def _repack_body(w_ref, gate_ref, up_ref):
    ci = lax.broadcasted_iota(i32, (MXU_COLS, MXU_COLS), 0)
    ji = lax.broadcasted_iota(i32, (MXU_COLS, MXU_COLS), 1)
    src = jnp.where(ji < LANES, 2 * ji, 2 * (ji - LANES) + 1)
    perm = jnp.where(ci == src, 1.0, 0.0).astype(bf16)
    for grp in range(w_ref.shape[1] // MXU_COLS):
        blk = w_ref[:, grp * MXU_COLS:(grp + 1) * MXU_COLS].astype(bf16)
        r = jnp.dot(blk, perm, preferred_element_type=f32)
        gate_ref[:, grp * LANES:(grp + 1) * LANES] = r[:, :LANES].astype(bf16)
        up_ref[:, grp * LANES:(grp + 1) * LANES] = r[:, LANES:].astype(bf16)


def _repack_expert_in(w):
    e, d, two_f = w.shape
    imap = lambda ei, ri: (ei, ri, 0)
    out = jax.ShapeDtypeStruct((e, d, two_f // 2), bf16)
    return pl.pallas_call(
        _repack_body,
        grid=(e, d // REPACK_ROWS),
        in_specs=[pl.BlockSpec((None, REPACK_ROWS, two_f), imap)],
        out_specs=[pl.BlockSpec((None, REPACK_ROWS, two_f // 2), imap)] * 2,
        out_shape=[out, out],
        compiler_params=_cparams(2),
        name="repack",
    )(w)


def _block_diag_dense(w):
    eye = jnp.eye(LRU_BLOCKS, dtype=w.dtype)
    return jnp.einsum('hij,hg->higj', w, eye).reshape(D_LRU, D_LRU)


def _prep(mix_norm_g, w_mix_in, lru_conv_w, lru_conv_b, lru_w_r, lru_b_r, lru_w_i, lru_b_i,
          lru_lambda, gla_w_alpha, gla_b_alpha, gla_norm_g, w_mix_out, ffn_norm_g,
          w_router, b_router, w_exp_in, b_exp_in, w_exp_out, b_exp_out, final_norm_g):
    p = {}
    p["mix_g"] = mix_norm_g[0].reshape(1, D_MODEL)
    p["w_in"] = jnp.pad(w_mix_in[0], ((0, 0), (0, D_IN_PAD - w_mix_in.shape[-1]))).astype(bf16)
    p["conv_w"] = lru_conv_w[0]
    p["conv_b"] = lru_conv_b[0].reshape(1, D_LRU)
    p["wg"] = jnp.stack([
        jnp.concatenate([_block_diag_dense(lru_w_r[0, d]), _block_diag_dense(lru_w_i[0, d])], axis=1)
        for d in range(2)]).astype(bf16)
    p["bg"] = jnp.concatenate([lru_b_r[0], lru_b_i[0]], axis=-1).reshape(2, 1, 2 * D_LRU)
    p["lam"] = lru_lambda[0].reshape(2, 1, D_LRU)
    zeros = jnp.zeros((GLA_RANK, HK), f32)
    p["wa"] = jnp.stack([jnp.concatenate([gla_w_alpha[0, 0], zeros], axis=0),
                         jnp.concatenate([zeros, gla_w_alpha[0, 1]], axis=0)])
    p["ba"] = gla_b_alpha[0].reshape(2, 1, HK)
    p["gn"] = gla_norm_g[0].reshape(1, HV)
    p["w_out"] = w_mix_out[0].astype(bf16)
    p["ffn_g"] = ffn_norm_g[0].reshape(1, D_MODEL)
    p["w_r"] = jnp.pad(w_router[0], ((0, 0), (0, LANES - N_EXPERTS)))
    p["b_r"] = jnp.pad(b_router[0], (0, LANES - N_EXPERTS), constant_values=-1e30).reshape(1, LANES)
    p["w_gate"], p["w_up"] = _repack_expert_in(w_exp_in[0])
    p["b_gate"] = b_exp_in[0, :, 0::2].reshape(N_EXPERTS, 1, D_FF)
    p["b_up"] = b_exp_in[0, :, 1::2].reshape(N_EXPERTS, 1, D_FF)
    p["w_eo"] = w_exp_out[0].astype(bf16)
    p["b_eo"] = b_exp_out[0].reshape(N_EXPERTS, 1, D_MODEL)
    p["final_g"] = final_norm_g.reshape(1, D_MODEL)
    return p


def _route_tables(route_i, counts, t):
    bm = BM_EXP
    n_assign = t * TOP_K
    n_blocks = n_assign // bm + N_EXPERTS
    cap = n_blocks * bm
    cnt = counts[0, :N_EXPERTS]
    padded = ((cnt + bm - 1) // bm) * bm
    pad_ends = jnp.cumsum(padded)
    pad_starts = pad_ends - padded
    e = route_i[:, :TOP_K]
    rank = route_i[:, TOP_K:2 * TOP_K]
    pos = (pad_starts[e] + rank).reshape(-1)
    slot_of_row = jnp.full((cap,), -1, i32).at[pos].set(
        jnp.arange(n_assign, dtype=i32), unique_indices=True)
    real = slot_of_row >= 0
    src_tok = jnp.where(real, slot_of_row // TOP_K, 0)
    spare = n_assign + jnp.arange(cap, dtype=i32) % bm
    dst_row = jnp.where(real, (slot_of_row % TOP_K) * t + slot_of_row // TOP_K, spare)
    idx = jnp.concatenate([src_tok.reshape(n_blocks, bm), dst_row.reshape(n_blocks, bm)],
                          axis=1) * SLAB
    block_start = jnp.arange(n_blocks, dtype=i32) * bm
    block_e = jnp.minimum(jnp.sum(pad_ends[None, :] <= block_start[:, None], axis=1),
                          N_EXPERTS - 1).astype(i32)
    n_real = jnp.clip(cnt[block_e] - (block_start - pad_starts[block_e]), 0, bm).astype(i32)
    return block_e, n_real, idx


def _trunk(x, p):
    b, s, _ = x.shape
    t = b * s
    x2 = x.reshape(t, D_MODEL)
    u, gate, q, k, v, g2, al = _in_proj(x2, s, p["mix_g"], p["w_in"], p["conv_w"], p["conv_b"])
    r3 = lambda a: a.reshape(b, s, a.shape[-1])
    hf, hb, of, ob = _mix(r3(u), r3(q), r3(k), r3(v), r3(al),
                          p["wg"], p["bg"], p["lam"], p["wa"], p["ba"])
    f2 = lambda a: a.reshape(t, a.shape[-1])
    x1, hn, route_i, route_g, counts = _out_proj(
        f2(hf), f2(hb), gate, f2(of), f2(ob), g2, x2,
        p["w_out"], p["gn"], p["ffn_g"], p["w_r"], p["b_r"])
    block_e, n_real, idx = _route_tables(route_i, counts, t)
    y4 = _experts(block_e, n_real, idx, hn, p["w_gate"], p["w_up"], p["b_gate"], p["b_up"],
                  p["w_eo"], p["b_eo"], t * TOP_K)
    y = _combine(y4, route_g, x1, p["final_g"])
    return y.reshape(b, s, D_MODEL)


def kernel(x_prompt, x_sample, mix_norm_g, w_mix_in, lru_conv_w, lru_conv_b, lru_w_r, lru_b_r,
           lru_w_i, lru_b_i, lru_lambda, gla_w_alpha, gla_b_alpha, gla_norm_g, w_mix_out,
           ffn_norm_g, w_router, b_router, w_exp_in, b_exp_in, w_exp_out, b_exp_out,
           final_norm_g):
    p = _prep(mix_norm_g, w_mix_in, lru_conv_w, lru_conv_b, lru_w_r, lru_b_r, lru_w_i, lru_b_i,
              lru_lambda, gla_w_alpha, gla_b_alpha, gla_norm_g, w_mix_out, ffn_norm_g,
              w_router, b_router, w_exp_in, b_exp_in, w_exp_out, b_exp_out, final_norm_g)
    return (_trunk(x_prompt, p), _trunk(x_sample, p))
```

```python
import functools

import jax
import jax.numpy as jnp
from jax import lax
from jax.experimental import pallas as pl
from jax.experimental.pallas import tpu as pltpu

f32 = jnp.float32
bf16 = jnp.bfloat16
i32 = jnp.int32

D_MODEL = 1024
D_LRU = 512
LRU_BLOCKS = 8
LRU_BW = 64
CONV_W = 4
LRU_C = 8.0
GLA_HEADS = 4
GLA_DK = 64
GLA_DV = 128
HK = GLA_HEADS * GLA_DK
HV = GLA_HEADS * GLA_DV
GLA_RANK = 16
GLA_TAU = 16.0
GLA_FAST_LIMIT = 60.0
GLA_SAFE_CHUNK = 16
N_EXPERTS = 32
TOP_K = 4
D_FF = 1024
SWIGLU_LIMIT = 7.0
SWIGLU_ALPHA = 1.702
EPS = 1e-6

LANES = 128
SUBLANES = 8
D_IN_PAD = 2688
VMEM_LIMIT = 56 * 1024 * 1024

TM_PROJ = 512
L_MIX = 256
BM_EXP = 256


def _cparams(n_axes):
    return pltpu.CompilerParams(
        dimension_semantics=("arbitrary",) * n_axes, vmem_limit_bytes=VMEM_LIMIT)


def _rms(x, g):
    ms = jnp.mean(x * x, axis=-1, keepdims=True)
    return x * lax.rsqrt(ms + EPS) * g


def _sigmoid(x):
    return 1.0 / (1.0 + jnp.exp(-x))


SLAB = D_MODEL // LANES


def _store_slabs(ref, x):
    n = x.shape[0]
    for s in range(SLAB):
        ref[pl.ds(s, n, stride=SLAB), :] = x[:, s * LANES:(s + 1) * LANES]


def _load_slabs(ref):
    n = ref.shape[0] // SLAB
    return jnp.concatenate([ref[pl.ds(s, n, stride=SLAB), :] for s in range(SLAB)], axis=1)


def _log_sigmoid(x):
    return jnp.minimum(x, 0.0) - jnp.log1p(jnp.exp(-jnp.abs(x)))


def _in_proj_body(tiles_per_seq, x_ref, xprev_ref, xnext_ref, g_ref, w_ref, cw_ref, cb_ref,
                  u_ref, gate_ref, q_ref, k_ref, v_ref, g2_ref, a_ref, ext_ref):
    i = pl.program_id(0)
    tm = x_ref.shape[0]
    g = g_ref[...]
    xn = _rms(x_ref[...], g).astype(bf16)

    def proj(lo, hi):
        return jnp.dot(xn, w_ref[:, lo:hi], preferred_element_type=f32)

    gate_ref[...] = proj(512, 1024).astype(bf16)
    q_ref[...] = proj(1024, 1280).astype(bf16)
    k_ref[...] = proj(1280, 1536).astype(bf16)
    v_ref[...] = proj(1536, 2048).astype(bf16)
    g2_ref[...] = proj(2048, 2560).astype(bf16)
    a_ref[...] = proj(2560, 2688)[:, :2 * GLA_RANK]

    xh = jnp.concatenate([xprev_ref[...], xnext_ref[...]], axis=0)
    hl = jnp.dot(_rms(xh, g).astype(bf16), w_ref[:, 0:512], preferred_element_type=f32)
    pos = i % tiles_per_seq
    ext_ref[0:SUBLANES, :] = jnp.where(pos == 0, 0.0, hl[:SUBLANES])
    ext_ref[SUBLANES:SUBLANES + tm, :] = proj(0, 512)
    ext_ref[SUBLANES + tm:, :] = jnp.where(pos == tiles_per_seq - 1, 0.0, hl[SUBLANES:])
    u = cb_ref[...]
    for j in range(CONV_W):
        u = u + cw_ref[j:j + 1, :] * ext_ref[pl.ds(SUBLANES - CONV_W // 2 + j, tm), :]
    u_ref[...] = u


def _in_proj(x2, seq, g, w, cw, cb):
    t = x2.shape[0]
    tm = min(TM_PROJ, seq)
    nt = t // tm
    r = tm // SUBLANES
    last_blk = t // SUBLANES - 1
    row = lambda i: (i, 0)
    const = lambda i: (0, 0)
    out_shapes = [
        jax.ShapeDtypeStruct((t, D_LRU), f32),
        jax.ShapeDtypeStruct((t, D_LRU), bf16),
        jax.ShapeDtypeStruct((t, HK), bf16),
        jax.ShapeDtypeStruct((t, HK), bf16),
        jax.ShapeDtypeStruct((t, HV), bf16),
        jax.ShapeDtypeStruct((t, HV), bf16),
        jax.ShapeDtypeStruct((t, 2 * GLA_RANK), f32),
    ]
    return pl.pallas_call(
        functools.partial(_in_proj_body, seq // tm),
        grid=(nt,),
        in_specs=[
            pl.BlockSpec((tm, D_MODEL), row),
            pl.BlockSpec((SUBLANES, D_MODEL), lambda i: (jnp.maximum(i * r - 1, 0), 0)),
            pl.BlockSpec((SUBLANES, D_MODEL), lambda i: (jnp.minimum((i + 1) * r, last_blk), 0)),
            pl.BlockSpec((1, D_MODEL), const),
            pl.BlockSpec((D_MODEL, D_IN_PAD), const),
            pl.BlockSpec((CONV_W, D_LRU), const),
            pl.BlockSpec((1, D_LRU), const),
        ],
        out_specs=[pl.BlockSpec((tm, s.shape[1]), row) for s in out_shapes],
        out_shape=out_shapes,
        scratch_shapes=[pltpu.VMEM((tm + 2 * SUBLANES, D_LRU), f32)],
        compiler_params=_cparams(1),
        name="in_proj",
    )(x2, x2, x2, g, w, cw, cb)


def _lru_tile(u_ref, d, reverse, wg_ref, bg_ref, lam_ref, a_scr, b_scr, hcar_ref, h_out_ref):
    n = u_ref.shape[0]
    u = u_ref[...]
    gates = jnp.dot(u.astype(bf16), wg_ref[d], preferred_element_type=f32) + bg_ref[d]
    r = _sigmoid(gates[:, :D_LRU])
    ig = _sigmoid(gates[:, D_LRU:])
    log_a = r * (LRU_C * _log_sigmoid(lam_ref[d]))
    a = jnp.exp(log_a)
    a_scr[...] = a
    b_scr[...] = jnp.sqrt(1.0 - a * a) * (ig * u)

    row = lax.broadcasted_iota(i32, (SUBLANES, D_LRU), 0)
    n_groups = n // SUBLANES

    def group(gi, carry):
        gidx = (n_groups - 1 - gi) if reverse else gi
        off = pl.multiple_of(gidx * SUBLANES, SUBLANES)
        a = a_scr[pl.ds(off, SUBLANES), :]
        b = b_scr[pl.ds(off, SUBLANES), :]
        for s in (1, 2, 4):
            if reverse:
                keep = row < SUBLANES - s
                shift = SUBLANES - s
            else:
                keep = row >= s
                shift = s
            a_nb = jnp.where(keep, pltpu.roll(a, shift, 0), 1.0)
            b_nb = jnp.where(keep, pltpu.roll(b, shift, 0), 0.0)
            b = a * b_nb + b
            a = a * a_nb
        h = a * carry + b
        a_scr[pl.ds(off, SUBLANES), :] = h
        edge = h[0:1, :] if reverse else h[SUBLANES - 1:SUBLANES, :]
        return jnp.broadcast_to(edge, (SUBLANES, D_LRU))

    hcar_ref[d] = lax.fori_loop(0, n_groups, group, hcar_ref[d])
    h_out_ref[...] = a_scr[...].astype(h_out_ref.dtype)


def _split_bf16(x):
    hi = x.astype(bf16)
    return hi, (x - hi.astype(f32)).astype(bf16)


def _state_diag_mask():
    return (lax.broadcasted_iota(i32, (HK, HV), 0) // GLA_DK
            == lax.broadcasted_iota(i32, (HK, HV), 1) // GLA_DV)


def _gla_fast(q, k, v, la, cum, tot, d, reverse, s_ref, o_out_ref):
    n = q.shape[0]
    ri = lax.broadcasted_iota(i32, (n, n), 0)
    ci = lax.broadcasted_iota(i32, (n, n), 1)
    causal = (ri < ci) if reverse else (ri >= ci)
    qa = q * jnp.exp(cum)
    kb = (k * jnp.exp(-cum)).astype(bf16)
    ke_t = (k * jnp.exp(tot - cum)).T.astype(bf16)

    state = s_ref[d]
    o_inter = jnp.dot(qa.astype(bf16), state.astype(bf16), preferred_element_type=f32)
    head_of_lane = lax.broadcasted_iota(i32, (1, HK), 1) // GLA_DK
    for h in range(GLA_HEADS):
        qh = jnp.where(head_of_lane == h, qa, 0.0).astype(bf16)
        s = lax.dot_general(qh, kb, (((1,), (1,)), ((), ())), preferred_element_type=f32)
        p = jnp.where(causal, s, 0.0).astype(bf16)
        lo, hi = h * GLA_DV, (h + 1) * GLA_DV
        o_h = jnp.dot(p, v[:, lo:hi], preferred_element_type=f32) + o_inter[:, lo:hi]
        o_out_ref[:, lo:hi] = o_h.astype(o_out_ref.dtype)

    kv = jnp.dot(ke_t, v, preferred_element_type=f32)
    dec = jnp.exp(jnp.sum(la.T, axis=1, keepdims=True))
    s_ref[d] = state * dec + jnp.where(_state_diag_mask(), kv, 0.0)


def _gla_safe(q, k, v, la, d, reverse, s_ref, o_out_ref):
    n = q.shape[0]
    c_len = GLA_SAFE_CHUNK
    n_chunks = n // c_len
    ri = lax.broadcasted_iota(i32, (n, n), 0)
    ci = lax.broadcasted_iota(i32, (n, n), 1)
    same = (ri // c_len) == (ci // c_len)
    incl = (ri <= ci) if reverse else (ri >= ci)
    la_hi, la_lo = _split_bf16(la)
    tri = jnp.where(same & incl, 1.0, 0.0).astype(bf16)
    ones = jnp.where(same, 1.0, 0.0).astype(bf16)
    lc = (jnp.dot(tri, la_hi, preferred_element_type=f32)
          + jnp.dot(tri, la_lo, preferred_element_type=f32))
    totc = (jnp.dot(ones, la_hi, preferred_element_type=f32)
            + jnp.dot(ones, la_lo, preferred_element_type=f32))
    vf = v.astype(f32)
    pos = lax.broadcasted_iota(i32, (n, 1), 0) % c_len
    diag = _state_diag_mask()
    head_sum = jnp.where(diag, 1.0, 0.0).astype(bf16)

    o = jnp.zeros((n, HV), f32)
    for delta in range(1 if reverse else 0, c_len):
        if delta == 0:
            k_j, lc_j, v_j = k, lc, vf
        else:
            shift = n - delta if reverse else delta
            k_j = pltpu.roll(k, shift, 0)
            lc_j = pltpu.roll(lc, shift, 0)
            v_j = pltpu.roll(vf, shift, 0)
        valid = (pos + delta < c_len) if reverse else (pos >= delta)
        dec = jnp.exp(jnp.where(valid, lc - lc_j, -1e30))
        scores = jnp.dot((q * k_j * dec).astype(bf16), head_sum, preferred_element_type=f32)
        o = o + scores * v_j

    ql = q * jnp.exp(lc)
    ke_t = (k * jnp.exp(totc - lc)).T.astype(bf16)
    la_t = la.T
    chunk_of_row = lax.broadcasted_iota(i32, (n, 1), 0) // c_len
    chunk_of_col = lax.broadcasted_iota(i32, (1, n), 1) // c_len

    def chunk(step, o_acc):
        c = (n_chunks - 1 - step) if reverse else step
        state = s_ref[d]
        rows = chunk_of_row == c
        qc = jnp.where(rows, ql, 0.0).astype(bf16)
        o_acc = o_acc + jnp.dot(qc, state.astype(bf16), preferred_element_type=f32)
        vc = jnp.where(rows, vf, 0.0).astype(bf16)
        kv = jnp.dot(ke_t, vc, preferred_element_type=f32)
        dec = jnp.exp(jnp.sum(jnp.where(chunk_of_col == c, la_t, 0.0), axis=1, keepdims=True))
        s_ref[d] = state * dec + jnp.where(diag, kv, 0.0)
        return o_acc

    o = lax.fori_loop(0, n_chunks, chunk, o)
    o_out_ref[...] = o.astype(o_out_ref.dtype)


def _gla_tile(q_ref, k_ref, v_ref, al_ref, d, reverse, wa_ref, ba_ref, s_ref, o_out_ref):
    n = q_ref.shape[0]
    z = jnp.dot(al_ref[...], wa_ref[d], preferred_element_type=f32,
                precision=lax.Precision.HIGHEST) + ba_ref[d]
    la = _log_sigmoid(z) * (1.0 / GLA_TAU)
    la_hi, la_lo = _split_bf16(la)
    ri = lax.broadcasted_iota(i32, (n, n), 0)
    ci = lax.broadcasted_iota(i32, (n, n), 1)
    tri = jnp.where((ri <= ci) if reverse else (ri >= ci), 1.0, 0.0).astype(bf16)
    cum = (jnp.dot(tri, la_hi, preferred_element_type=f32)
           + jnp.dot(tri, la_lo, preferred_element_type=f32))
    tot = cum[0:1, :] if reverse else cum[n - 1:n, :]
    q = q_ref[...].astype(f32) * (GLA_DK ** -0.5)
    k = k_ref[...].astype(f32)
    v = v_ref[...]
    strong_decay = jnp.min(tot) < -GLA_FAST_LIMIT

    @pl.when(jnp.logical_not(strong_decay))
    def _():
        _gla_fast(q, k, v, la, cum, tot, d, reverse, s_ref, o_out_ref)

    @pl.when(strong_decay)
    def _():
        _gla_safe(q, k, v, la, d, reverse, s_ref, o_out_ref)


def _mix_body(uf_ref, qf_ref, kf_ref, vf_ref, af_ref, ub_ref, qb_ref, kb_ref, vb_ref, ab_ref,
              wg_ref, bg_ref, lam_ref, wa_ref, ba_ref,
              hf_ref, hb_ref, of_ref, ob_ref,
              a_scr, b_scr, hcar_ref, s_ref):
    @pl.when(pl.program_id(1) == 0)
    def _():
        hcar_ref[...] = jnp.zeros_like(hcar_ref)
        s_ref[...] = jnp.zeros_like(s_ref)

    _lru_tile(uf_ref, 0, False, wg_ref, bg_ref, lam_ref, a_scr, b_scr, hcar_ref, hf_ref)
    _lru_tile(ub_ref, 1, True, wg_ref, bg_ref, lam_ref, a_scr, b_scr, hcar_ref, hb_ref)
    _gla_tile(qf_ref, kf_ref, vf_ref, af_ref, 0, False, wa_ref, ba_ref, s_ref, of_ref)
    _gla_tile(qb_ref, kb_ref, vb_ref, ab_ref, 1, True, wa_ref, ba_ref, s_ref, ob_ref)


def _mix(u, q, k, v, al, wg, bg, lam, wa, ba):
    b, s, _ = u.shape
    n = min(L_MIX, s)
    nt = s // n
    fwd = lambda bi, j: (bi, j, 0)
    bwd = lambda bi, j: (bi, nt - 1 - j, 0)
    const3 = lambda bi, j: (0, 0, 0)

    def tile_specs(imap):
        return [pl.BlockSpec((None, n, c), imap) for c in (D_LRU, HK, HK, HV, 2 * GLA_RANK)]

    out_shape = [jax.ShapeDtypeStruct((b, s, c), bf16) for c in (D_LRU, D_LRU, HV, HV)]
    return pl.pallas_call(
        _mix_body,
        grid=(b, nt),
        in_specs=tile_specs(fwd) + tile_specs(bwd) + [
            pl.BlockSpec(wg.shape, const3),
            pl.BlockSpec(bg.shape, const3),
            pl.BlockSpec(lam.shape, const3),
            pl.BlockSpec(wa.shape, const3),
            pl.BlockSpec(ba.shape, const3),
        ],
        out_specs=[
            pl.BlockSpec((None, n, D_LRU), fwd),
            pl.BlockSpec((None, n, D_LRU), bwd),
            pl.BlockSpec((None, n, HV), fwd),
            pl.BlockSpec((None, n, HV), bwd),
        ],
        out_shape=out_shape,
        scratch_shapes=[
            pltpu.VMEM((n, D_LRU), f32),
            pltpu.VMEM((n, D_LRU), f32),
            pltpu.VMEM((2, SUBLANES, D_LRU), f32),
            pltpu.VMEM((2, HK, HV), f32),
        ],
        compiler_params=_cparams(2),
        name="mix",
    )(u, q, k, v, al, u, q, k, v, al, wg, bg, lam, wa, ba)


def _out_proj_body(hf_ref, hb_ref, gate_ref, of_ref, ob_ref, g2_ref, x_ref,
                   wout_ref, gn_ref, fg_ref, wr_ref, br_ref,
                   x1_ref, hn_ref, ri_ref, rg_ref, cnt_ref, cnt_scr):
    i = pl.program_id(0)
    tm = x_ref.shape[0]

    @pl.when(i == 0)
    def _():
        cnt_scr[...] = jnp.zeros_like(cnt_scr)

    hs = hf_ref[...].astype(f32) + hb_ref[...].astype(f32)
    gt = gate_ref[...].astype(f32)
    gelu = 0.5 * gt * (1.0 + jnp.tanh(0.7978845608028654 * (gt + 0.044715 * (gt * gt * gt))))
    lru_out = (hs * gelu).astype(bf16)

    o = of_ref[...].astype(f32) + ob_ref[...].astype(f32)
    g2 = g2_ref[...].astype(f32)
    silu = g2 * _sigmoid(g2)
    mix = jnp.dot(lru_out, wout_ref[0:D_LRU, :], preferred_element_type=f32)
    for h in range(GLA_HEADS):
        lo, hi = h * GLA_DV, (h + 1) * GLA_DV
        oh = o[:, lo:hi]
        on = oh * lax.rsqrt(jnp.mean(oh * oh, axis=-1, keepdims=True) + EPS)
        gla_h = (on * gn_ref[:, lo:hi] * silu[:, lo:hi]).astype(bf16)
        mix = mix + jnp.dot(gla_h, wout_ref[D_LRU + lo:D_LRU + hi, :],
                            preferred_element_type=f32)
    x1 = x_ref[...] + mix
    x1_ref[...] = x1
    hn = _rms(x1, fg_ref[...])
    _store_slabs(hn_ref, hn)

    logits = jnp.dot(hn, wr_ref[...], preferred_element_type=f32,
                     precision=lax.Precision.HIGHEST) + br_ref[...]
    lane = lax.broadcasted_iota(i32, (tm, LANES), 1)
    lane_f = lane.astype(f32)
    sel_idx, sel_val = [], []
    member = jnp.zeros((tm, LANES), f32)
    for _ in range(TOP_K):
        m = jnp.max(logits, axis=-1, keepdims=True)
        idx = jnp.min(jnp.where(logits == m, lane_f, float(LANES)), axis=-1, keepdims=True)
        hit = lane_f == idx
        member = jnp.where(hit, 1.0, member)
        logits = jnp.where(hit, -jnp.inf, logits)
        sel_idx.append(idx)
        sel_val.append(m)
    ex = [jnp.exp(mv - sel_val[0]) for mv in sel_val]
    inv = 1.0 / (ex[0] + ex[1] + ex[2] + ex[3])

    ri = lax.broadcasted_iota(i32, (tm, tm), 0)
    ci = lax.broadcasted_iota(i32, (tm, tm), 1)
    before = jnp.where(ri > ci, 1.0, 0.0).astype(bf16)
    prefix = jnp.dot(before, member.astype(bf16), preferred_element_type=f32) + cnt_scr[...]
    cnt_new = cnt_scr[...] + jnp.sum(member, axis=0, keepdims=True)
    cnt_scr[...] = cnt_new
    cnt_ref[...] = cnt_new.astype(i32)

    route_i = jnp.zeros((tm, LANES), f32)
    route_g = jnp.zeros((tm, LANES), f32)
    for kk in range(TOP_K):
        rank = jnp.sum(jnp.where(lane_f == sel_idx[kk], prefix, 0.0), axis=-1, keepdims=True)
        route_i = jnp.where(lane == kk, sel_idx[kk], route_i)
        route_i = jnp.where(lane == TOP_K + kk, rank, route_i)
        route_g = jnp.where(lane == kk, ex[kk] * inv, route_g)
    ri_ref[...] = route_i.astype(i32)
    rg_ref[...] = route_g


def _out_proj(hf, hb, gate, of, ob, g2, x2, wout, gn, fg, wr, br):
    t = x2.shape[0]
    tm = min(TM_PROJ, t)
    row = lambda i: (i, 0)
    const = lambda i: (0, 0)
    out_shape = [
        jax.ShapeDtypeStruct((t, D_MODEL), f32),
        jax.ShapeDtypeStruct((t * SLAB, LANES), f32),
        jax.ShapeDtypeStruct((t, LANES), i32),
        jax.ShapeDtypeStruct((t, LANES), f32),
        jax.ShapeDtypeStruct((1, LANES), i32),
    ]
    return pl.pallas_call(
        _out_proj_body,
        grid=(t // tm,),
        in_specs=[
            pl.BlockSpec((tm, D_LRU), row),
            pl.BlockSpec((tm, D_LRU), row),
            pl.BlockSpec((tm, D_LRU), row),
            pl.BlockSpec((tm, HV), row),
            pl.BlockSpec((tm, HV), row),
            pl.BlockSpec((tm, HV), row),
            pl.BlockSpec((tm, D_MODEL), row),
            pl.BlockSpec((D_MODEL, D_MODEL), const),
            pl.BlockSpec((1, HV), const),
            pl.BlockSpec((1, D_MODEL), const),
            pl.BlockSpec((D_MODEL, LANES), const),
            pl.BlockSpec((1, LANES), const),
        ],
        out_specs=[
            pl.BlockSpec((tm, D_MODEL), row),
            pl.BlockSpec((tm * SLAB, LANES), row),
            pl.BlockSpec((tm, LANES), row),
            pl.BlockSpec((tm, LANES), row),
            pl.BlockSpec((1, LANES), const),
        ],
        out_shape=out_shape,
        scratch_shapes=[pltpu.VMEM((1, LANES), f32)],
        compiler_params=_cparams(1),
        name="out_proj",
    )(hf, hb, gate, of, ob, g2, x2, wout, gn, fg, wr, br)


def _slab_copy(src, src_sub, dst, dst_sub, sem):
    return pltpu.make_async_copy(src.at[pl.ds(src_sub, SLAB)], dst.at[pl.ds(dst_sub, SLAB)], sem)


def _experts_body(be_ref, nreal_ref, idx_hbm, hn_hbm, wg_ref, wu_ref, bgt_ref, bup_ref,
                  wo_ref, bo_ref, y4_hbm, idx_smem, xbuf0, xbuf1, ybuf0, ybuf1, isem, gsem, ssem):
    n = pl.program_id(0)
    n_blocks = pl.num_programs(0)
    bm = xbuf0.shape[0] // SLAB
    spare_sub = y4_hbm.shape[0] - bm * SLAB

    def used(m):
        inside = (m >= 0) & (m < n_blocks)
        return jnp.where(inside, nreal_ref[jnp.clip(m, 0, n_blocks - 1)], 0) > 0

    def idx_copy(m):
        ring = m % IDX_RING
        return pltpu.make_async_copy(idx_hbm.at[jnp.minimum(m, n_blocks - 1)], idx_smem.at[ring],
                                     isem.at[ring])

    def issue_gather(m, xb, sem):
        ring = m % IDX_RING
        for r in range(bm):
            src = pl.multiple_of(idx_smem[ring, r], SLAB)
            _slab_copy(hn_hbm, src, xb, r * SLAB, sem).start()

    def wait_gather(xb, sem):
        pltpu.make_async_copy(hn_hbm.at[pl.ds(0, bm * SLAB)], xb, sem).wait()

    def issue_scatter(m, yb):
        ring = m % IDX_RING
        for r in range(bm):
            dst = pl.multiple_of(idx_smem[ring, bm + r], SLAB)
            _slab_copy(yb, r * SLAB, y4_hbm, dst, ssem).start()

    def wait_scatter(yb):
        pltpu.make_async_copy(yb, y4_hbm.at[pl.ds(0, bm * SLAB)], ssem).wait()

    def compute(xb, yb):
        x = _load_slabs(xb).astype(bf16)
        gate = jnp.dot(x, wg_ref[...], preferred_element_type=f32) + bgt_ref[...]
        up = jnp.dot(x, wu_ref[...], preferred_element_type=f32) + bup_ref[...]
        gate = jnp.minimum(gate, SWIGLU_LIMIT)
        up = jnp.clip(up, -SWIGLU_LIMIT, SWIGLU_LIMIT)
        glu = gate * _sigmoid(SWIGLU_ALPHA * gate)
        act = ((up + 1.0) * glu).astype(bf16)
        _store_slabs(yb, jnp.dot(act, wo_ref[...], preferred_element_type=f32) + bo_ref[...])

    bufs = ((xbuf0, ybuf0, gsem.at[0], xbuf1, ybuf1, gsem.at[1]),
            (xbuf1, ybuf1, gsem.at[1], xbuf0, ybuf0, gsem.at[0]))

    def step(parity, has_prev):
        x_cur, y_cur, g_cur, x_alt, y_alt, g_alt = bufs[parity]
        idx_copy(n + 1).wait()
        issue_gather(n + 1, x_alt, g_alt)
        if has_prev:
            issue_scatter(n - 1, y_alt)
        idx_copy(n + 2).start()
        wait_gather(x_cur, g_cur)
        compute(x_cur, y_cur)
        if has_prev:
            wait_scatter(y_alt)

    def drain(parity):
        x_cur, _, g_cur, _, y_alt, _ = bufs[parity]
        idx_copy(n + 1).wait()
        wait_gather(x_cur, g_cur)
        issue_scatter(n - 1, y_alt)
        wait_scatter(y_alt)
        fill = pltpu.make_async_copy(y_alt, y4_hbm.at[pl.ds(spare_sub, bm * SLAB)], ssem)
        fill.start()
        fill.wait()

    @pl.when((n == 0) & used(0))
    def _():
        first = idx_copy(0)
        first.start()
        first.wait()
        idx_copy(1).start()
        issue_gather(0, xbuf0, gsem.at[0])
        step(0, False)

    for parity in range(2):
        @pl.when((n > 0) & (n % 2 == parity) & used(n))
        def _():
            step(parity, True)

        @pl.when((n % 2 == parity) & used(n - 1) & jnp.logical_not(used(n)))
        def _():
            drain(parity)


IDX_RING = 4


def _experts(block_e, n_real, idx, hn, wg, wu, bgt, bup, wo, bo, n_out):
    n_blocks, two_bm = idx.shape
    bm = two_bm // 2
    wmap = lambda n, be, nu: (be[n], 0, 0)
    grid_spec = pltpu.PrefetchScalarGridSpec(
        num_scalar_prefetch=2,
        grid=(n_blocks,),
        in_specs=[
            pl.BlockSpec(memory_space=pl.ANY),
            pl.BlockSpec(memory_space=pl.ANY),
            pl.BlockSpec((None, D_MODEL, D_FF), wmap),
            pl.BlockSpec((None, D_MODEL, D_FF), wmap),
            pl.BlockSpec((None, 1, D_FF), wmap),
            pl.BlockSpec((None, 1, D_FF), wmap),
            pl.BlockSpec((None, D_FF, D_MODEL), wmap),
            pl.BlockSpec((None, 1, D_MODEL), wmap),
        ],
        out_specs=pl.BlockSpec(memory_space=pl.ANY),
        scratch_shapes=[
            pltpu.SMEM((IDX_RING, two_bm), i32),
            pltpu.VMEM((bm * SLAB, LANES), f32),
            pltpu.VMEM((bm * SLAB, LANES), f32),
            pltpu.VMEM((bm * SLAB, LANES), f32),
            pltpu.VMEM((bm * SLAB, LANES), f32),
            pltpu.SemaphoreType.DMA((IDX_RING,)),
            pltpu.SemaphoreType.DMA((2,)),
            pltpu.SemaphoreType.DMA,
        ],
    )
    return pl.pallas_call(
        _experts_body,
        grid_spec=grid_spec,
        out_shape=jax.ShapeDtypeStruct(((n_out + bm) * SLAB, LANES), f32),
        compiler_params=_cparams(1),
        name="experts",
    )(block_e, n_real, idx, hn, wg, wu, bgt, bup, wo, bo)


def _combine_body(y0_ref, y1_ref, y2_ref, y3_ref, rg_ref, x1_ref, g_ref, out_ref):
    acc = x1_ref[...]
    for kk, y_ref in enumerate((y0_ref, y1_ref, y2_ref, y3_ref)):
        acc = acc + rg_ref[:, kk:kk + 1] * _load_slabs(y_ref)
    out_ref[...] = _rms(acc, g_ref[...])


def _combine(y4, rg, x1, g):
    t = x1.shape[0]
    tm = min(TM_PROJ, t)
    row = lambda i: (i, 0)
    slot_specs = [pl.BlockSpec((tm * SLAB, LANES),
                               functools.partial(lambda kk, i: (kk * (t // tm) + i, 0), kk))
                  for kk in range(TOP_K)]
    return pl.pallas_call(
        _combine_body,
        grid=(t // tm,),
        in_specs=slot_specs + [
            pl.BlockSpec((tm, LANES), row),
            pl.BlockSpec((tm, D_MODEL), row),
            pl.BlockSpec((1, D_MODEL), lambda i: (0, 0)),
        ],
        out_specs=pl.BlockSpec((tm, D_MODEL), row),
        out_shape=jax.ShapeDtypeStruct((t, D_MODEL), f32),
        compiler_params=_cparams(1),
        name="combine",
    )(y4, y4, y4, y4, rg, x1, g)


REPACK_ROWS = 512
MXU_COLS = 256


def _repack_body(w_ref, gate_ref, up_ref):
    ci = lax.broadcasted_iota(i32, (MXU_COLS, MXU_COLS), 0)
    ji = lax.broadcasted_iota(i32, (MXU_COLS, MXU_COLS), 1)
    src = jnp.where(ji < LANES, 2 * ji, 2 * (ji - LANES) + 1)
    perm = jnp.where(ci == src, 1.0, 0.0).astype(bf16)
    for grp in range(w_ref.shape[1] // MXU_COLS):
        blk = w_ref[:, grp * MXU_COLS:(grp + 1) * MXU_COLS].astype(bf16)
        r = jnp.dot(blk, perm, preferred_element_type=f32)
        gate_ref[:, grp * LANES:(grp + 1) * LANES] = r[:, :LANES].astype(bf16)
        up_ref[:, grp * LANES:(grp + 1) * LANES] = r[:, LANES:].astype(bf16)


def _repack_expert_in(w):
    e, d, two_f = w.shape
    imap = lambda ei, ri: (ei, ri, 0)
    out = jax.ShapeDtypeStruct((e, d, two_f // 2), bf16)
    return pl.pallas_call(
        _repack_body,
        grid=(e, d // REPACK_ROWS),
        in_specs=[pl.BlockSpec((None, REPACK_ROWS, two_f), imap)],
        out_specs=[pl.BlockSpec((None, REPACK_ROWS, two_f // 2), imap)] * 2,
        out_shape=[out, out],
        compiler_params=_cparams(2),
        name="repack",
    )(w)


def _block_diag_dense(w):
    eye = jnp.eye(LRU_BLOCKS, dtype=w.dtype)
    return jnp.einsum('hij,hg->higj', w, eye).reshape(D_LRU, D_LRU)


def _prep(mix_norm_g, w_mix_in, lru_conv_w, lru_conv_b, lru_w_r, lru_b_r, lru_w_i, lru_b_i,
          lru_lambda, gla_w_alpha, gla_b_alpha, gla_norm_g, w_mix_out, ffn_norm_g,
          w_router, b_router, w_exp_in, b_exp_in, w_exp_out, b_exp_out, final_norm_g):
    p = {}
    p["mix_g"] = mix_norm_g[0].reshape(1, D_MODEL)
    p["w_in"] = jnp.pad(w_mix_in[0], ((0, 0), (0, D_IN_PAD - w_mix_in.shape[-1]))).astype(bf16)
    p["conv_w"] = lru_conv_w[0]
    p["conv_b"] = lru_conv_b[0].reshape(1, D_LRU)
    p["wg"] = jnp.stack([
        jnp.concatenate([_block_diag_dense(lru_w_r[0, d]), _block_diag_dense(lru_w_i[0, d])], axis=1)
        for d in range(2)]).astype(bf16)
    p["bg"] = jnp.concatenate([lru_b_r[0], lru_b_i[0]], axis=-1).reshape(2, 1, 2 * D_LRU)
    p["lam"] = lru_lambda[0].reshape(2, 1, D_LRU)
    zeros = jnp.zeros((GLA_RANK, HK), f32)
    p["wa"] = jnp.stack([jnp.concatenate([gla_w_alpha[0, 0], zeros], axis=0),
                         jnp.concatenate([zeros, gla_w_alpha[0, 1]], axis=0)])
    p["ba"] = gla_b_alpha[0].reshape(2, 1, HK)
    p["gn"] = gla_norm_g[0].reshape(1, HV)
    p["w_out"] = w_mix_out[0].astype(bf16)
    p["ffn_g"] = ffn_norm_g[0].reshape(1, D_MODEL)
    p["w_r"] = jnp.pad(w_router[0], ((0, 0), (0, LANES - N_EXPERTS)))
    p["b_r"] = jnp.pad(b_router[0], (0, LANES - N_EXPERTS), constant_values=-1e30).reshape(1, LANES)
    p["w_gate"], p["w_up"] = _repack_expert_in(w_exp_in[0])
    p["b_gate"] = b_exp_in[0, :, 0::2].reshape(N_EXPERTS, 1, D_FF)
    p["b_up"] = b_exp_in[0, :, 1::2].reshape(N_EXPERTS, 1, D_FF)
    p["w_eo"] = w_exp_out[0].astype(bf16)
    p["b_eo"] = b_exp_out[0].reshape(N_EXPERTS, 1, D_MODEL)
    p["final_g"] = final_norm_g.reshape(1, D_MODEL)
    return p


def _route_tables(route_i, counts, t):
    bm = BM_EXP
    n_assign = t * TOP_K
    n_blocks = n_assign // bm + N_EXPERTS
    cap = n_blocks * bm
    cnt = counts[0, :N_EXPERTS]
    padded = ((cnt + bm - 1) // bm) * bm
    pad_ends = jnp.cumsum(padded)
    pad_starts = pad_ends - padded
    e = route_i[:, :TOP_K]
    rank = route_i[:, TOP_K:2 * TOP_K]
    pos = (pad_starts[e] + rank).reshape(-1)
    slot_of_row = jnp.full((cap,), -1, i32).at[pos].set(
        jnp.arange(n_assign, dtype=i32), unique_indices=True)
    real = slot_of_row >= 0
    src_tok = jnp.where(real, slot_of_row // TOP_K, 0)
    spare = n_assign + jnp.arange(cap, dtype=i32) % bm
    dst_row = jnp.where(real, (slot_of_row % TOP_K) * t + slot_of_row // TOP_K, spare)
    idx = jnp.concatenate([src_tok.reshape(n_blocks, bm), dst_row.reshape(n_blocks, bm)],
                          axis=1) * SLAB
    block_start = jnp.arange(n_blocks, dtype=i32) * bm
    block_e = jnp.minimum(jnp.sum(pad_ends[None, :] <= block_start[:, None], axis=1),
                          N_EXPERTS - 1).astype(i32)
    n_real = jnp.clip(cnt[block_e] - (block_start - pad_starts[block_e]), 0, bm).astype(i32)
    return block_e, n_real, idx


def _trunk(x, p):
    b, s, _ = x.shape
    t = b * s
    x2 = x.reshape(t, D_MODEL)
    u, gate, q, k, v, g2, al = _in_proj(x2, s, p["mix_g"], p["w_in"], p["conv_w"], p["conv_b"])
    r3 = lambda a: a.reshape(b, s, a.shape[-1])
    hf, hb, of, ob = _mix(r3(u), r3(q), r3(k), r3(v), r3(al),
                          p["wg"], p["bg"], p["lam"], p["wa"], p["ba"])
    f2 = lambda a: a.reshape(t, a.shape[-1])
    x1, hn, route_i, route_g, counts = _out_proj(
        f2(hf), f2(hb), gate, f2(of), f2(ob), g2, x2,
        p["w_out"], p["gn"], p["ffn_g"], p["w_r"], p["b_r"])
    block_e, n_real, idx = _route_tables(route_i, counts, t)
    y4 = _experts(block_e, n_real, idx, hn, p["w_gate"], p["w_up"], p["b_gate"], p["b_up"],
                  p["w_eo"], p["b_eo"], t * TOP_K)
    y = _combine(y4, route_g, x1, p["final_g"])
    return y.reshape(b, s, D_MODEL)


def kernel(x_prompt, x_sample, mix_norm_g, w_mix_in, lru_conv_w, lru_conv_b, lru_w_r, lru_b_r,
           lru_w_i, lru_b_i, lru_lambda, gla_w_alpha, gla_b_alpha, gla_norm_g, w_mix_out,
           ffn_norm_g, w_router, b_router, w_exp_in, b_exp_in, w_exp_out, b_exp_out,
           final_norm_g):
    p = _prep(mix_norm_g, w_mix_in, lru_conv_w, lru_conv_b, lru_w_r, lru_b_r, lru_w_i, lru_b_i,
              lru_lambda, gla_w_alpha, gla_b_alpha, gla_norm_g, w_mix_out, ffn_norm_g,
              w_router, b_router, w_exp_in, b_exp_in, w_exp_out, b_exp_out, final_norm_g)
    return (_trunk(x_prompt, p), _trunk(x_sample, p))
```

```python
import functools

import jax
import jax.numpy as jnp
from jax import lax
from jax.experimental import pallas as pl
from jax.experimental.pallas import tpu as pltpu

f32 = jnp.float32
bf16 = jnp.bfloat16
i32 = jnp.int32

D_MODEL = 1024
D_LRU = 512
LRU_BLOCKS = 8
LRU_BW = 64
CONV_W = 4
LRU_C = 8.0
GLA_HEADS = 4
GLA_DK = 64
GLA_DV = 128
HK = GLA_HEADS * GLA_DK
HV = GLA_HEADS * GLA_DV
GLA_RANK = 16
GLA_TAU = 16.0
GLA_FAST_LIMIT = 60.0
GLA_SAFE_CHUNK = 16
N_EXPERTS = 32
TOP_K = 4
D_FF = 1024
SWIGLU_LIMIT = 7.0
SWIGLU_ALPHA = 1.702
EPS = 1e-6

LANES = 128
SUBLANES = 8
D_IN_PAD = 2688
VMEM_LIMIT = 56 * 1024 * 1024

TM_PROJ = 512
L_MIX = 256
BM_EXP = 256


def _cparams(n_axes):
    return pltpu.CompilerParams(
        dimension_semantics=("arbitrary",) * n_axes, vmem_limit_bytes=VMEM_LIMIT)


def _rms(x, g):
    ms = jnp.mean(x * x, axis=-1, keepdims=True)
    return x * lax.rsqrt(ms + EPS) * g


def _sigmoid(x):
    return 1.0 / (1.0 + jnp.exp(-x))


SLAB = D_MODEL // LANES


def _store_slabs(ref, x):
    n = x.shape[0]
    for s in range(SLAB):
        ref[pl.ds(s, n, stride=SLAB), :] = x[:, s * LANES:(s + 1) * LANES]


def _load_slabs(ref):
    n = ref.shape[0] // SLAB
    return jnp.concatenate([ref[pl.ds(s, n, stride=SLAB), :] for s in range(SLAB)], axis=1)


def _log_sigmoid(x):
    return jnp.minimum(x, 0.0) - jnp.log1p(jnp.exp(-jnp.abs(x)))


def _in_proj_body(tiles_per_seq, x_ref, xprev_ref, xnext_ref, g_ref, w_ref, cw_ref, cb_ref,
                  u_ref, gate_ref, q_ref, k_ref, v_ref, g2_ref, a_ref, ext_ref):
    i = pl.program_id(0)
    tm = x_ref.shape[0]
    g = g_ref[...]
    xn = _rms(x_ref[...], g).astype(bf16)

    def proj(lo, hi):
        return jnp.dot(xn, w_ref[:, lo:hi], preferred_element_type=f32)

    gate_ref[...] = proj(512, 1024).astype(bf16)
    q_ref[...] = proj(1024, 1280).astype(bf16)
    k_ref[...] = proj(1280, 1536).astype(bf16)
    v_ref[...] = proj(1536, 2048).astype(bf16)
    g2_ref[...] = proj(2048, 2560).astype(bf16)
    a_ref[...] = proj(2560, 2688)[:, :2 * GLA_RANK]

    xh = jnp.concatenate([xprev_ref[...], xnext_ref[...]], axis=0)
    hl = jnp.dot(_rms(xh, g).astype(bf16), w_ref[:, 0:512], preferred_element_type=f32)
    pos = i % tiles_per_seq
    ext_ref[0:SUBLANES, :] = jnp.where(pos == 0, 0.0, hl[:SUBLANES])
    ext_ref[SUBLANES:SUBLANES + tm, :] = proj(0, 512)
    ext_ref[SUBLANES + tm:, :] = jnp.where(pos == tiles_per_seq - 1, 0.0, hl[SUBLANES:])
    u = cb_ref[...]
    for j in range(CONV_W):
        u = u + cw_ref[j:j + 1, :] * ext_ref[pl.ds(SUBLANES - CONV_W // 2 + j, tm), :]
    u_ref[...] = u


def _in_proj(x2, seq, g, w, cw, cb):
    t = x2.shape[0]
    tm = min(TM_PROJ, seq)
    nt = t // tm
    r = tm // SUBLANES
    last_blk = t // SUBLANES - 1
    row = lambda i: (i, 0)
    const = lambda i: (0, 0)
    out_shapes = [
        jax.ShapeDtypeStruct((t, D_LRU), f32),
        jax.ShapeDtypeStruct((t, D_LRU), bf16),
        jax.ShapeDtypeStruct((t, HK), bf16),
        jax.ShapeDtypeStruct((t, HK), bf16),
        jax.ShapeDtypeStruct((t, HV), bf16),
        jax.ShapeDtypeStruct((t, HV), bf16),
        jax.ShapeDtypeStruct((t, 2 * GLA_RANK), f32),
    ]
    return pl.pallas_call(
        functools.partial(_in_proj_body, seq // tm),
        grid=(nt,),
        in_specs=[
            pl.BlockSpec((tm, D_MODEL), row),
            pl.BlockSpec((SUBLANES, D_MODEL), lambda i: (jnp.maximum(i * r - 1, 0), 0)),
            pl.BlockSpec((SUBLANES, D_MODEL), lambda i: (jnp.minimum((i + 1) * r, last_blk), 0)),
            pl.BlockSpec((1, D_MODEL), const),
            pl.BlockSpec((D_MODEL, D_IN_PAD), const),
            pl.BlockSpec((CONV_W, D_LRU), const),
            pl.BlockSpec((1, D_LRU), const),
        ],
        out_specs=[pl.BlockSpec((tm, s.shape[1]), row) for s in out_shapes],
        out_shape=out_shapes,
        scratch_shapes=[pltpu.VMEM((tm + 2 * SUBLANES, D_LRU), f32)],
        compiler_params=_cparams(1),
        name="in_proj",
    )(x2, x2, x2, g, w, cw, cb)


def _lru_tile(u_ref, d, reverse, wg_ref, bg_ref, lam_ref, a_scr, b_scr, hcar_ref, h_out_ref):
    n = u_ref.shape[0]
    u = u_ref[...]
    gates = jnp.dot(u.astype(bf16), wg_ref[d], preferred_element_type=f32) + bg_ref[d]
    r = _sigmoid(gates[:, :D_LRU])
    ig = _sigmoid(gates[:, D_LRU:])
    log_a = r * (LRU_C * _log_sigmoid(lam_ref[d]))
    a = jnp.exp(log_a)
    a_scr[...] = a
    b_scr[...] = jnp.sqrt(1.0 - a * a) * (ig * u)

    row = lax.broadcasted_iota(i32, (SUBLANES, D_LRU), 0)
    n_groups = n // SUBLANES

    def group(gi, carry):
        gidx = (n_groups - 1 - gi) if reverse else gi
        off = pl.multiple_of(gidx * SUBLANES, SUBLANES)
        a = a_scr[pl.ds(off, SUBLANES), :]
        b = b_scr[pl.ds(off, SUBLANES), :]
        for s in (1, 2, 4):
            if reverse:
                keep = row < SUBLANES - s
                shift = SUBLANES - s
            else:
                keep = row >= s
                shift = s
            a_nb = jnp.where(keep, pltpu.roll(a, shift, 0), 1.0)
            b_nb = jnp.where(keep, pltpu.roll(b, shift, 0), 0.0)
            b = a * b_nb + b
            a = a * a_nb
        h = a * carry + b
        a_scr[pl.ds(off, SUBLANES), :] = h
        edge = h[0:1, :] if reverse else h[SUBLANES - 1:SUBLANES, :]
        return jnp.broadcast_to(edge, (SUBLANES, D_LRU))

    hcar_ref[d] = lax.fori_loop(0, n_groups, group, hcar_ref[d])
    h_out_ref[...] = a_scr[...].astype(h_out_ref.dtype)


def _split_bf16(x):
    hi = x.astype(bf16)
    return hi, (x - hi.astype(f32)).astype(bf16)


def _state_diag_mask():
    return (lax.broadcasted_iota(i32, (HK, HV), 0) // GLA_DK
            == lax.broadcasted_iota(i32, (HK, HV), 1) // GLA_DV)


def _gla_fast(q, k, v, la, cum, tot, d, reverse, s_ref, o_out_ref):
    n = q.shape[0]
    ri = lax.broadcasted_iota(i32, (n, n), 0)
    ci = lax.broadcasted_iota(i32, (n, n), 1)
    causal = (ri < ci) if reverse else (ri >= ci)
    qa = q * jnp.exp(cum)
    kb = (k * jnp.exp(-cum)).astype(bf16)
    ke_t = (k * jnp.exp(tot - cum)).T.astype(bf16)

    state = s_ref[d]
    o_inter = jnp.dot(qa.astype(bf16), state.astype(bf16), preferred_element_type=f32)
    head_of_lane = lax.broadcasted_iota(i32, (1, HK), 1) // GLA_DK
    for h in range(GLA_HEADS):
        qh = jnp.where(head_of_lane == h, qa, 0.0).astype(bf16)
        s = lax.dot_general(qh, kb, (((1,), (1,)), ((), ())), preferred_element_type=f32)
        p = jnp.where(causal, s, 0.0).astype(bf16)
        lo, hi = h * GLA_DV, (h + 1) * GLA_DV
        o_h = jnp.dot(p, v[:, lo:hi], preferred_element_type=f32) + o_inter[:, lo:hi]
        o_out_ref[:, lo:hi] = o_h.astype(o_out_ref.dtype)

    kv = jnp.dot(ke_t, v, preferred_element_type=f32)
    dec = jnp.exp(jnp.sum(la.T, axis=1, keepdims=True))
    s_ref[d] = state * dec + jnp.where(_state_diag_mask(), kv, 0.0)


def _gla_safe(q, k, v, la, d, reverse, s_ref, o_out_ref):
    n = q.shape[0]
    c_len = GLA_SAFE_CHUNK
    n_chunks = n // c_len
    ri = lax.broadcasted_iota(i32, (n, n), 0)
    ci = lax.broadcasted_iota(i32, (n, n), 1)
    same = (ri // c_len) == (ci // c_len)
    incl = (ri <= ci) if reverse else (ri >= ci)
    la_hi, la_lo = _split_bf16(la)
    tri = jnp.where(same & incl, 1.0, 0.0).astype(bf16)
    ones = jnp.where(same, 1.0, 0.0).astype(bf16)
    lc = (jnp.dot(tri, la_hi, preferred_element_type=f32)
          + jnp.dot(tri, la_lo, preferred_element_type=f32))
    totc = (jnp.dot(ones, la_hi, preferred_element_type=f32)
            + jnp.dot(ones, la_lo, preferred_element_type=f32))
    vf = v.astype(f32)
    pos = lax.broadcasted_iota(i32, (n, 1), 0) % c_len
    diag = _state_diag_mask()
    head_sum = jnp.where(diag, 1.0, 0.0).astype(bf16)

    o = jnp.zeros((n, HV), f32)
    for delta in range(1 if reverse else 0, c_len):
        if delta == 0:
            k_j, lc_j, v_j = k, lc, vf
        else:
            shift = n - delta if reverse else delta
            k_j = pltpu.roll(k, shift, 0)
            lc_j = pltpu.roll(lc, shift, 0)
            v_j = pltpu.roll(vf, shift, 0)
        valid = (pos + delta < c_len) if reverse else (pos >= delta)
        dec = jnp.exp(jnp.where(valid, lc - lc_j, -1e30))
        scores = jnp.dot((q * k_j * dec).astype(bf16), head_sum, preferred_element_type=f32)
        o = o + scores * v_j

    ql = q * jnp.exp(lc)
    ke_t = (k * jnp.exp(totc - lc)).T.astype(bf16)
    la_t = la.T
    chunk_of_row = lax.broadcasted_iota(i32, (n, 1), 0) // c_len
    chunk_of_col = lax.broadcasted_iota(i32, (1, n), 1) // c_len

    def chunk(step, o_acc):
        c = (n_chunks - 1 - step) if reverse else step
        state = s_ref[d]
        rows = chunk_of_row == c
        qc = jnp.where(rows, ql, 0.0).astype(bf16)
        o_acc = o_acc + jnp.dot(qc, state.astype(bf16), preferred_element_type=f32)
        vc = jnp.where(rows, vf, 0.0).astype(bf16)
        kv = jnp.dot(ke_t, vc, preferred_element_type=f32)
        dec = jnp.exp(jnp.sum(jnp.where(chunk_of_col == c, la_t, 0.0), axis=1, keepdims=True))
        s_ref[d] = state * dec + jnp.where(diag, kv, 0.0)
        return o_acc

    o = lax.fori_loop(0, n_chunks, chunk, o)
    o_out_ref[...] = o.astype(o_out_ref.dtype)


def _gla_tile(q_ref, k_ref, v_ref, al_ref, d, reverse, wa_ref, ba_ref, s_ref, o_out_ref):
    n = q_ref.shape[0]
    z = jnp.dot(al_ref[...], wa_ref[d], preferred_element_type=f32,
                precision=lax.Precision.HIGHEST) + ba_ref[d]
    la = _log_sigmoid(z) * (1.0 / GLA_TAU)
    la_hi, la_lo = _split_bf16(la)
    ri = lax.broadcasted_iota(i32, (n, n), 0)
    ci = lax.broadcasted_iota(i32, (n, n), 1)
    tri = jnp.where((ri <= ci) if reverse else (ri >= ci), 1.0, 0.0).astype(bf16)
    cum = (jnp.dot(tri, la_hi, preferred_element_type=f32)
           + jnp.dot(tri, la_lo, preferred_element_type=f32))
    tot = cum[0:1, :] if reverse else cum[n - 1:n, :]
    q = q_ref[...].astype(f32) * (GLA_DK ** -0.5)
    k = k_ref[...].astype(f32)
    v = v_ref[...]
    strong_decay = jnp.min(tot) < -GLA_FAST_LIMIT

    @pl.when(jnp.logical_not(strong_decay))
    def _():
        _gla_fast(q, k, v, la, cum, tot, d, reverse, s_ref, o_out_ref)

    @pl.when(strong_decay)
    def _():
        _gla_safe(q, k, v, la, d, reverse, s_ref, o_out_ref)


def _mix_body(uf_ref, qf_ref, kf_ref, vf_ref, af_ref, ub_ref, qb_ref, kb_ref, vb_ref, ab_ref,
              wg_ref, bg_ref, lam_ref, wa_ref, ba_ref,
              hf_ref, hb_ref, of_ref, ob_ref,
              a_scr, b_scr, hcar_ref, s_ref):
    @pl.when(pl.program_id(1) == 0)
    def _():
        hcar_ref[...] = jnp.zeros_like(hcar_ref)
        s_ref[...] = jnp.zeros_like(s_ref)

    _lru_tile(uf_ref, 0, False, wg_ref, bg_ref, lam_ref, a_scr, b_scr, hcar_ref, hf_ref)
    _lru_tile(ub_ref, 1, True, wg_ref, bg_ref, lam_ref, a_scr, b_scr, hcar_ref, hb_ref)
    _gla_tile(qf_ref, kf_ref, vf_ref, af_ref, 0, False, wa_ref, ba_ref, s_ref, of_ref)
    _gla_tile(qb_ref, kb_ref, vb_ref, ab_ref, 1, True, wa_ref, ba_ref, s_ref, ob_ref)


def _mix(u, q, k, v, al, wg, bg, lam, wa, ba):
    b, s, _ = u.shape
    n = min(L_MIX, s)
    nt = s // n
    fwd = lambda bi, j: (bi, j, 0)
    bwd = lambda bi, j: (bi, nt - 1 - j, 0)
    const3 = lambda bi, j: (0, 0, 0)

    def tile_specs(imap):
        return [pl.BlockSpec((None, n, c), imap) for c in (D_LRU, HK, HK, HV, 2 * GLA_RANK)]

    out_shape = [jax.ShapeDtypeStruct((b, s, c), bf16) for c in (D_LRU, D_LRU, HV, HV)]
    return pl.pallas_call(
        _mix_body,
        grid=(b, nt),
        in_specs=tile_specs(fwd) + tile_specs(bwd) + [
            pl.BlockSpec(wg.shape, const3),
            pl.BlockSpec(bg.shape, const3),
            pl.BlockSpec(lam.shape, const3),
            pl.BlockSpec(wa.shape, const3),
            pl.BlockSpec(ba.shape, const3),
        ],
        out_specs=[
            pl.BlockSpec((None, n, D_LRU), fwd),
            pl.BlockSpec((None, n, D_LRU), bwd),
            pl.BlockSpec((None, n, HV), fwd),
            pl.BlockSpec((None, n, HV), bwd),
        ],
        out_shape=out_shape,
        scratch_shapes=[
            pltpu.VMEM((n, D_LRU), f32),
            pltpu.VMEM((n, D_LRU), f32),
            pltpu.VMEM((2, SUBLANES, D_LRU), f32),
            pltpu.VMEM((2, HK, HV), f32),
        ],
        compiler_params=_cparams(2),
        name="mix",
    )(u, q, k, v, al, u, q, k, v, al, wg, bg, lam, wa, ba)


def _out_proj_body(hf_ref, hb_ref, gate_ref, of_ref, ob_ref, g2_ref, x_ref,
                   wout_ref, gn_ref, fg_ref, wr_ref, br_ref,
                   x1_ref, hn_ref, ri_ref, rg_ref, cnt_ref, cnt_scr):
    i = pl.program_id(0)
    tm = x_ref.shape[0]

    @pl.when(i == 0)
    def _():
        cnt_scr[...] = jnp.zeros_like(cnt_scr)

    hs = hf_ref[...].astype(f32) + hb_ref[...].astype(f32)
    gt = gate_ref[...].astype(f32)
    gelu = 0.5 * gt * (1.0 + jnp.tanh(0.7978845608028654 * (gt + 0.044715 * (gt * gt * gt))))
    lru_out = (hs * gelu).astype(bf16)

    o = of_ref[...].astype(f32) + ob_ref[...].astype(f32)
    g2 = g2_ref[...].astype(f32)
    silu = g2 * _sigmoid(g2)
    cat = [lru_out]
    for h in range(GLA_HEADS):
        lo, hi = h * GLA_DV, (h + 1) * GLA_DV
        oh = o[:, lo:hi]
        on = oh * lax.rsqrt(jnp.mean(oh * oh, axis=-1, keepdims=True) + EPS)
        cat.append((on * gn_ref[:, lo:hi] * silu[:, lo:hi]).astype(bf16))
    mix = jnp.dot(jnp.concatenate(cat, axis=1), wout_ref[...], preferred_element_type=f32)
    x1 = x_ref[...] + mix
    x1_ref[...] = x1
    hn = _rms(x1, fg_ref[...])
    _store_slabs(hn_ref, hn)

    hn_hi, hn_lo = _split_bf16(hn)
    logits = (jnp.dot(hn_hi, wr_ref[0], preferred_element_type=f32)
              + jnp.dot(hn_hi, wr_ref[1], preferred_element_type=f32)
              + jnp.dot(hn_lo, wr_ref[0], preferred_element_type=f32)) + br_ref[...]
    lane = lax.broadcasted_iota(i32, (tm, LANES), 1)
    lane_f = lane.astype(f32)
    sel_idx, sel_val = [], []
    member = jnp.zeros((tm, LANES), f32)
    for _ in range(TOP_K):
        m = jnp.max(logits, axis=-1, keepdims=True)
        idx = jnp.min(jnp.where(logits == m, lane_f, float(LANES)), axis=-1, keepdims=True)
        hit = lane_f == idx
        member = jnp.where(hit, 1.0, member)
        logits = jnp.where(hit, -jnp.inf, logits)
        sel_idx.append(idx)
        sel_val.append(m)
    ex = [jnp.exp(mv - sel_val[0]) for mv in sel_val]
    inv = 1.0 / (ex[0] + ex[1] + ex[2] + ex[3])

    ri = lax.broadcasted_iota(i32, (tm, tm), 0)
    ci = lax.broadcasted_iota(i32, (tm, tm), 1)
    before = jnp.where(ri > ci, 1.0, 0.0).astype(bf16)
    prefix = jnp.dot(before, member.astype(bf16), preferred_element_type=f32) + cnt_scr[...]
    cnt_new = cnt_scr[...] + jnp.sum(member, axis=0, keepdims=True)
    cnt_scr[...] = cnt_new
    cnt_ref[...] = cnt_new.astype(i32)

    route_i = jnp.zeros((tm, LANES), f32)
    route_g = jnp.zeros((tm, LANES), f32)
    for kk in range(TOP_K):
        rank = jnp.sum(jnp.where(lane_f == sel_idx[kk], prefix, 0.0), axis=-1, keepdims=True)
        route_i = jnp.where(lane == kk, sel_idx[kk], route_i)
        route_i = jnp.where(lane == TOP_K + kk, rank, route_i)
        route_g = jnp.where(lane == kk, ex[kk] * inv, route_g)
    ri_ref[...] = route_i.astype(i32)
    rg_ref[...] = route_g


def _out_proj(hf, hb, gate, of, ob, g2, x2, wout, gn, fg, wr, br):
    t = x2.shape[0]
    tm = min(TM_PROJ, t)
    row = lambda i: (i, 0)
    const = lambda i: (0, 0)
    out_shape = [
        jax.ShapeDtypeStruct((t, D_MODEL), f32),
        jax.ShapeDtypeStruct((t * SLAB, LANES), f32),
        jax.ShapeDtypeStruct((t, LANES), i32),
        jax.ShapeDtypeStruct((t, LANES), f32),
        jax.ShapeDtypeStruct((1, LANES), i32),
    ]
    return pl.pallas_call(
        _out_proj_body,
        grid=(t // tm,),
        in_specs=[
            pl.BlockSpec((tm, D_LRU), row),
            pl.BlockSpec((tm, D_LRU), row),
            pl.BlockSpec((tm, D_LRU), row),
            pl.BlockSpec((tm, HV), row),
            pl.BlockSpec((tm, HV), row),
            pl.BlockSpec((tm, HV), row),
            pl.BlockSpec((tm, D_MODEL), row),
            pl.BlockSpec((D_MODEL, D_MODEL), const),
            pl.BlockSpec((1, HV), const),
            pl.BlockSpec((1, D_MODEL), const),
            pl.BlockSpec((2, D_MODEL, LANES), lambda i: (0, 0, 0)),
            pl.BlockSpec((1, LANES), const),
        ],
        out_specs=[
            pl.BlockSpec((tm, D_MODEL), row),
            pl.BlockSpec((tm * SLAB, LANES), row),
            pl.BlockSpec((tm, LANES), row),
            pl.BlockSpec((tm, LANES), row),
            pl.BlockSpec((1, LANES), const),
        ],
        out_shape=out_shape,
        scratch_shapes=[pltpu.VMEM((1, LANES), f32)],
        compiler_params=_cparams(1),
        name="out_proj",
    )(hf, hb, gate, of, ob, g2, x2, wout, gn, fg, wr, br)


def _slab_copy(src, src_sub, dst, dst_sub, sem):
    return pltpu.make_async_copy(src.at[pl.ds(src_sub, SLAB)], dst.at[pl.ds(dst_sub, SLAB)], sem)


def _experts_body(be_ref, nreal_ref, src_hbm, dst_hbm, hn_hbm, wg_ref, wu_ref, bgt_ref, bup_ref,
                  wo_ref, bo_ref, y4_hbm, src0, src1, dst0, dst1, xbuf0, xbuf1, ybuf0, ybuf1,
                  isem, dsem, gsem, ssem):
    n = pl.program_id(0)
    n_blocks = pl.num_programs(0)
    bm = xbuf0.shape[0] // SLAB
    spare_sub = y4_hbm.shape[0] - bm * SLAB

    def used(m):
        inside = (m >= 0) & (m < n_blocks)
        return jnp.where(inside, nreal_ref[jnp.clip(m, 0, n_blocks - 1)], 0) > 0

    def src_copy(m, smem, sem):
        return pltpu.make_async_copy(src_hbm.at[jnp.minimum(m, n_blocks - 1)], smem, sem)

    def dst_copy(m, smem, sem):
        return pltpu.make_async_copy(dst_hbm.at[jnp.maximum(m, 0)], smem, sem)

    def issue_gather(smem, xb, sem):
        for r in range(bm):
            _slab_copy(hn_hbm, pl.multiple_of(smem[r], SLAB), xb, r * SLAB, sem).start()

    def wait_gather(xb, sem):
        pltpu.make_async_copy(hn_hbm.at[pl.ds(0, bm * SLAB)], xb, sem).wait()

    def issue_scatter(smem, yb):
        for r in range(bm):
            _slab_copy(yb, r * SLAB, y4_hbm, pl.multiple_of(smem[r], SLAB), ssem).start()

    def wait_scatter(yb):
        pltpu.make_async_copy(yb, y4_hbm.at[pl.ds(0, bm * SLAB)], ssem).wait()

    def compute(xb, yb):
        x = _load_slabs(xb).astype(bf16)
        gate = jnp.dot(x, wg_ref[...], preferred_element_type=f32) + bgt_ref[...]
        up = jnp.dot(x, wu_ref[...], preferred_element_type=f32) + bup_ref[...]
        gate = jnp.minimum(gate, SWIGLU_LIMIT)
        up = jnp.clip(up, -SWIGLU_LIMIT, SWIGLU_LIMIT)
        glu = gate * _sigmoid(SWIGLU_ALPHA * gate)
        act = ((up + 1.0) * glu).astype(bf16)
        _store_slabs(yb, jnp.dot(act, wo_ref[...], preferred_element_type=f32) + bo_ref[...])

    bufs = (dict(x_cur=xbuf0, y_cur=ybuf0, g_cur=gsem.at[0], x_alt=xbuf1, y_alt=ybuf1,
                 g_alt=gsem.at[1], src_cur=src0, dst_cur=dst0, src_alt=src1, dst_alt=dst1,
                 i_cur=isem.at[0], d_cur=dsem.at[0], i_alt=isem.at[1], d_alt=dsem.at[1]),
            dict(x_cur=xbuf1, y_cur=ybuf1, g_cur=gsem.at[1], x_alt=xbuf0, y_alt=ybuf0,
                 g_alt=gsem.at[0], src_cur=src1, dst_cur=dst1, src_alt=src0, dst_alt=dst0,
                 i_cur=isem.at[1], d_cur=dsem.at[1], i_alt=isem.at[0], d_alt=dsem.at[0]))

    def step(parity, has_prev):
        b = bufs[parity]
        src_copy(n + 1, b["src_alt"], b["i_alt"]).wait()
        issue_gather(b["src_alt"], b["x_alt"], b["g_alt"])
        if has_prev:
            dst_copy(n - 1, b["dst_alt"], b["d_alt"]).wait()
            issue_scatter(b["dst_alt"], b["y_alt"])
        src_copy(n + 2, b["src_cur"], b["i_cur"]).start()
        dst_copy(n, b["dst_cur"], b["d_cur"]).start()
        wait_gather(b["x_cur"], b["g_cur"])
        compute(b["x_cur"], b["y_cur"])
        if has_prev:
            wait_scatter(b["y_alt"])

    def drain(parity):
        b = bufs[parity]
        src_copy(n + 1, b["src_alt"], b["i_alt"]).wait()
        wait_gather(b["x_cur"], b["g_cur"])
        dst_copy(n - 1, b["dst_alt"], b["d_alt"]).wait()
        issue_scatter(b["dst_alt"], b["y_alt"])
        wait_scatter(b["y_alt"])
        fill = pltpu.make_async_copy(b["y_alt"], y4_hbm.at[pl.ds(spare_sub, bm * SLAB)], ssem)
        fill.start()
        fill.wait()

    @pl.when((n == 0) & used(0))
    def _():
        first = src_copy(0, src0, isem.at[0])
        first.start()
        first.wait()
        src_copy(1, src1, isem.at[1]).start()
        issue_gather(src0, xbuf0, gsem.at[0])
        step(0, False)

    for parity in range(2):
        @pl.when((n > 0) & (n % 2 == parity) & used(n))
        def _():
            step(parity, True)

        @pl.when((n % 2 == parity) & used(n - 1) & jnp.logical_not(used(n)))
        def _():
            drain(parity)


def _experts(block_e, n_real, src_idx, dst_idx, hn, wg, wu, bgt, bup, wo, bo, n_out):
    n_blocks, bm = src_idx.shape
    wmap = lambda n, be, nu: (be[n], 0, 0)
    grid_spec = pltpu.PrefetchScalarGridSpec(
        num_scalar_prefetch=2,
        grid=(n_blocks,),
        in_specs=[
            pl.BlockSpec(memory_space=pl.ANY),
            pl.BlockSpec(memory_space=pl.ANY),
            pl.BlockSpec(memory_space=pl.ANY),
            pl.BlockSpec((None, D_MODEL, D_FF), wmap),
            pl.BlockSpec((None, D_MODEL, D_FF), wmap),
            pl.BlockSpec((None, 1, D_FF), wmap),
            pl.BlockSpec((None, 1, D_FF), wmap),
            pl.BlockSpec((None, D_FF, D_MODEL), wmap),
            pl.BlockSpec((None, 1, D_MODEL), wmap),
        ],
        out_specs=pl.BlockSpec(memory_space=pl.ANY),
        scratch_shapes=(
            [pltpu.SMEM((bm,), i32)] * 4
            + [pltpu.VMEM((bm * SLAB, LANES), f32)] * 4
            + [pltpu.SemaphoreType.DMA((2,))] * 3
            + [pltpu.SemaphoreType.DMA]),
    )
    return pl.pallas_call(
        _experts_body,
        grid_spec=grid_spec,
        out_shape=jax.ShapeDtypeStruct(((n_out + bm) * SLAB, LANES), f32),
        compiler_params=_cparams(1),
        name="experts",
    )(block_e, n_real, src_idx, dst_idx, hn, wg, wu, bgt, bup, wo, bo)


def _combine_body(y0_ref, y1_ref, y2_ref, y3_ref, rg_ref, x1_ref, g_ref, out_ref):
    acc = x1_ref[...]
    for kk, y_ref in enumerate((y0_ref, y1_ref, y2_ref, y3_ref)):
        acc = acc + rg_ref[:, kk:kk + 1] * _load_slabs(y_ref)
    out_ref[...] = _rms(acc, g_ref[...])


def _combine(y4, rg, x1, g):
    t = x1.shape[0]
    tm = min(TM_PROJ, t)
    row = lambda i: (i, 0)
    slot_specs = [pl.BlockSpec((tm * SLAB, LANES),
                               functools.partial(lambda kk, i: (kk * (t // tm) + i, 0), kk))
                  for kk in range(TOP_K)]
    return pl.pallas_call(
        _combine_body,
        grid=(t // tm,),
        in_specs=slot_specs + [
            pl.BlockSpec((tm, LANES), row),
            pl.BlockSpec((tm, D_MODEL), row),
            pl.BlockSpec((1, D_MODEL), lambda i: (0, 0)),
        ],
        out_specs=pl.BlockSpec((tm, D_MODEL), row),
        out_shape=jax.ShapeDtypeStruct((t, D_MODEL), f32),
        compiler_params=_cparams(1),
        name="combine",
    )(y4, y4, y4, y4, rg, x1, g)


REPACK_ROWS = 512
MXU_COLS = 256


def _repack_body(w_ref, gate_ref, up_ref):
    ci = lax.broadcasted_iota(i32, (MXU_COLS, MXU_COLS), 0)
    ji = lax.broadcasted_iota(i32, (MXU_COLS, MXU_COLS), 1)
    src = jnp.where(ji < LANES, 2 * ji, 2 * (ji - LANES) + 1)
    perm = jnp.where(ci == src, 1.0, 0.0).astype(bf16)
    for grp in range(w_ref.shape[1] // MXU_COLS):
        blk = w_ref[:, grp * MXU_COLS:(grp + 1) * MXU_COLS].astype(bf16)
        r = jnp.dot(blk, perm, preferred_element_type=f32)
        gate_ref[:, grp * LANES:(grp + 1) * LANES] = r[:, :LANES].astype(bf16)
        up_ref[:, grp * LANES:(grp + 1) * LANES] = r[:, LANES:].astype(bf16)


def _repack_expert_in(w):
    e, d, two_f = w.shape
    imap = lambda ei, ri: (ei, ri, 0)
    out = jax.ShapeDtypeStruct((e, d, two_f // 2), bf16)
    return pl.pallas_call(
        _repack_body,
        grid=(e, d // REPACK_ROWS),
        in_specs=[pl.BlockSpec((None, REPACK_ROWS, two_f), imap)],
        out_specs=[pl.BlockSpec((None, REPACK_ROWS, two_f // 2), imap)] * 2,
        out_shape=[out, out],
        compiler_params=_cparams(2),
        name="repack",
    )(w)


def _block_diag_dense(w):
    eye = jnp.eye(LRU_BLOCKS, dtype=w.dtype)
    return jnp.einsum('hij,hg->higj', w, eye).reshape(D_LRU, D_LRU)


def _prep(mix_norm_g, w_mix_in, lru_conv_w, lru_conv_b, lru_w_r, lru_b_r, lru_w_i, lru_b_i,
          lru_lambda, gla_w_alpha, gla_b_alpha, gla_norm_g, w_mix_out, ffn_norm_g,
          w_router, b_router, w_exp_in, b_exp_in, w_exp_out, b_exp_out, final_norm_g):
    p = {}
    p["mix_g"] = mix_norm_g[0].reshape(1, D_MODEL)
    p["w_in"] = jnp.pad(w_mix_in[0], ((0, 0), (0, D_IN_PAD - w_mix_in.shape[-1]))).astype(bf16)
    p["conv_w"] = lru_conv_w[0]
    p["conv_b"] = lru_conv_b[0].reshape(1, D_LRU)
    p["wg"] = jnp.stack([
        jnp.concatenate([_block_diag_dense(lru_w_r[0, d]), _block_diag_dense(lru_w_i[0, d])], axis=1)
        for d in range(2)]).astype(bf16)
    p["bg"] = jnp.concatenate([lru_b_r[0], lru_b_i[0]], axis=-1).reshape(2, 1, 2 * D_LRU)
    p["lam"] = lru_lambda[0].reshape(2, 1, D_LRU)
    zeros = jnp.zeros((GLA_RANK, HK), f32)
    p["wa"] = jnp.stack([jnp.concatenate([gla_w_alpha[0, 0], zeros], axis=0),
                         jnp.concatenate([zeros, gla_w_alpha[0, 1]], axis=0)])
    p["ba"] = gla_b_alpha[0].reshape(2, 1, HK)
    p["gn"] = gla_norm_g[0].reshape(1, HV)
    p["w_out"] = w_mix_out[0].astype(bf16)
    p["ffn_g"] = ffn_norm_g[0].reshape(1, D_MODEL)
    w_r = jnp.pad(w_router[0], ((0, 0), (0, LANES - N_EXPERTS)))
    w_r_hi = w_r.astype(bf16)
    p["w_r"] = jnp.stack([w_r_hi, (w_r - w_r_hi.astype(f32)).astype(bf16)])
    p["b_r"] = jnp.pad(b_router[0], (0, LANES - N_EXPERTS), constant_values=-1e30).reshape(1, LANES)
    p["w_gate"], p["w_up"] = _repack_expert_in(w_exp_in[0])
    p["b_gate"] = b_exp_in[0, :, 0::2].reshape(N_EXPERTS, 1, D_FF)
    p["b_up"] = b_exp_in[0, :, 1::2].reshape(N_EXPERTS, 1, D_FF)
    p["w_eo"] = w_exp_out[0].astype(bf16)
    p["b_eo"] = b_exp_out[0].reshape(N_EXPERTS, 1, D_MODEL)
    p["final_g"] = final_norm_g.reshape(1, D_MODEL)
    return p


def _route_tables(route_i, counts, t):
    bm = BM_EXP
    n_assign = t * TOP_K
    n_blocks = n_assign // bm + N_EXPERTS
    cap = n_blocks * bm
    cnt = counts[0, :N_EXPERTS]
    padded = ((cnt + bm - 1) // bm) * bm
    pad_ends = jnp.cumsum(padded)
    pad_starts = pad_ends - padded
    e = route_i[:, :TOP_K]
    rank = route_i[:, TOP_K:2 * TOP_K]
    pos = (pad_starts[e] + rank).reshape(-1)
    slot_of_row = jnp.full((cap,), -1, i32).at[pos].set(
        jnp.arange(n_assign, dtype=i32), unique_indices=True)
    real = slot_of_row >= 0
    src_tok = jnp.where(real, slot_of_row // TOP_K, 0)
    spare = n_assign + jnp.arange(cap, dtype=i32) % bm
    dst_row = jnp.where(real, (slot_of_row % TOP_K) * t + slot_of_row // TOP_K, spare)
    src_idx = src_tok.reshape(n_blocks, bm) * SLAB
    dst_idx = dst_row.reshape(n_blocks, bm) * SLAB
    block_start = jnp.arange(n_blocks, dtype=i32) * bm
    block_e = jnp.minimum(jnp.sum(pad_ends[None, :] <= block_start[:, None], axis=1),
                          N_EXPERTS - 1).astype(i32)
    n_real = jnp.clip(cnt[block_e] - (block_start - pad_starts[block_e]), 0, bm).astype(i32)
    return block_e, n_real, src_idx, dst_idx


def _trunk(x, p):
    b, s, _ = x.shape
    t = b * s
    x2 = x.reshape(t, D_MODEL)
    u, gate, q, k, v, g2, al = _in_proj(x2, s, p["mix_g"], p["w_in"], p["conv_w"], p["conv_b"])
    r3 = lambda a: a.reshape(b, s, a.shape[-1])
    hf, hb, of, ob = _mix(r3(u), r3(q), r3(k), r3(v), r3(al),
                          p["wg"], p["bg"], p["lam"], p["wa"], p["ba"])
    f2 = lambda a: a.reshape(t, a.shape[-1])
    x1, hn, route_i, route_g, counts = _out_proj(
        f2(hf), f2(hb), gate, f2(of), f2(ob), g2, x2,
        p["w_out"], p["gn"], p["ffn_g"], p["w_r"], p["b_r"])
    block_e, n_real, src_idx, dst_idx = _route_tables(route_i, counts, t)
    y4 = _experts(block_e, n_real, src_idx, dst_idx, hn, p["w_gate"], p["w_up"], p["b_gate"], p["b_up"],
                  p["w_eo"], p["b_eo"], t * TOP_K)
    y = _combine(y4, route_g, x1, p["final_g"])
    return y.reshape(b, s, D_MODEL)


def kernel(x_prompt, x_sample, mix_norm_g, w_mix_in, lru_conv_w, lru_conv_b, lru_w_r, lru_b_r,
           lru_w_i, lru_b_i, lru_lambda, gla_w_alpha, gla_b_alpha, gla_norm_g, w_mix_out,
           ffn_norm_g, w_router, b_router, w_exp_in, b_exp_in, w_exp_out, b_exp_out,
           final_norm_g):
    p = _prep(mix_norm_g, w_mix_in, lru_conv_w, lru_conv_b, lru_w_r, lru_b_r, lru_w_i, lru_b_i,
              lru_lambda, gla_w_alpha, gla_b_alpha, gla_norm_g, w_mix_out, ffn_norm_g,
              w_router, b_router, w_exp_in, b_exp_in, w_exp_out, b_exp_out, final_norm_g)
    return (_trunk(x_prompt, p), _trunk(x_sample, p))
```

```python
import functools

import jax
import jax.numpy as jnp
from jax import lax
from jax.experimental import pallas as pl
from jax.experimental.pallas import tpu as pltpu

f32 = jnp.float32
bf16 = jnp.bfloat16
i32 = jnp.int32

D_MODEL = 1024
D_LRU = 512
LRU_BLOCKS = 8
LRU_BW = 64
CONV_W = 4
LRU_C = 8.0
GLA_HEADS = 4
GLA_DK = 64
GLA_DV = 128
HK = GLA_HEADS * GLA_DK
HV = GLA_HEADS * GLA_DV
GLA_RANK = 16
GLA_TAU = 16.0
GLA_FAST_LIMIT = 60.0
GLA_SAFE_CHUNK = 16
N_EXPERTS = 32
TOP_K = 4
D_FF = 1024
SWIGLU_LIMIT = 7.0
SWIGLU_ALPHA = 1.702
EPS = 1e-6

LANES = 128
SUBLANES = 8
D_IN_PAD = 2688
VMEM_LIMIT = 56 * 1024 * 1024

TM_PROJ = 512
L_MIX = 256
BM_EXP = 256


def _cparams(n_axes):
    return pltpu.CompilerParams(
        dimension_semantics=("arbitrary",) * n_axes, vmem_limit_bytes=VMEM_LIMIT)


def _rms(x, g):
    ms = jnp.mean(x * x, axis=-1, keepdims=True)
    return x * lax.rsqrt(ms + EPS) * g


def _sigmoid(x):
    return 1.0 / (1.0 + jnp.exp(-x))


SLAB = D_MODEL // LANES


def _store_slabs(ref, x):
    n = x.shape[0]
    for s in range(SLAB):
        ref[pl.ds(s, n, stride=SLAB), :] = x[:, s * LANES:(s + 1) * LANES]


def _load_slabs(ref):
    n = ref.shape[0] // SLAB
    return jnp.concatenate([ref[pl.ds(s, n, stride=SLAB), :] for s in range(SLAB)], axis=1)


def _log_sigmoid(x):
    return jnp.minimum(x, 0.0) - jnp.log1p(jnp.exp(-jnp.abs(x)))


def _in_proj_body(tiles_per_seq, x_ref, xprev_ref, xnext_ref, g_ref, w_ref, cw_ref, cb_ref,
                  u_ref, gate_ref, q_ref, k_ref, v_ref, g2_ref, a_ref, ext_ref):
    i = pl.program_id(0)
    tm = x_ref.shape[0]
    g = g_ref[...]
    xn = _rms(x_ref[...], g).astype(bf16)

    def proj(lo, hi):
        return jnp.dot(xn, w_ref[:, lo:hi], preferred_element_type=f32)

    gate_ref[...] = proj(512, 1024).astype(bf16)
    q_ref[...] = proj(1024, 1280).astype(bf16)
    k_ref[...] = proj(1280, 1536).astype(bf16)
    v_ref[...] = proj(1536, 2048).astype(bf16)
    g2_ref[...] = proj(2048, 2560).astype(bf16)
    a_ref[...] = proj(2560, 2688)[:, :2 * GLA_RANK]

    xh = jnp.concatenate([xprev_ref[...], xnext_ref[...]], axis=0)
    hl = jnp.dot(_rms(xh, g).astype(bf16), w_ref[:, 0:512], preferred_element_type=f32)
    pos = i % tiles_per_seq
    ext_ref[0:SUBLANES, :] = jnp.where(pos == 0, 0.0, hl[:SUBLANES])
    ext_ref[SUBLANES:SUBLANES + tm, :] = proj(0, 512)
    ext_ref[SUBLANES + tm:, :] = jnp.where(pos == tiles_per_seq - 1, 0.0, hl[SUBLANES:])
    u = cb_ref[...]
    for j in range(CONV_W):
        u = u + cw_ref[j:j + 1, :] * ext_ref[pl.ds(SUBLANES - CONV_W // 2 + j, tm), :]
    u_ref[...] = u


def _in_proj(x2, seq, g, w, cw, cb):
    t = x2.shape[0]
    tm = min(TM_PROJ, seq)
    nt = t // tm
    r = tm // SUBLANES
    last_blk = t // SUBLANES - 1
    row = lambda i: (i, 0)
    const = lambda i: (0, 0)
    out_shapes = [
        jax.ShapeDtypeStruct((t, D_LRU), f32),
        jax.ShapeDtypeStruct((t, D_LRU), bf16),
        jax.ShapeDtypeStruct((t, HK), bf16),
        jax.ShapeDtypeStruct((t, HK), bf16),
        jax.ShapeDtypeStruct((t, HV), bf16),
        jax.ShapeDtypeStruct((t, HV), bf16),
        jax.ShapeDtypeStruct((t, 2 * GLA_RANK), f32),
    ]
    return pl.pallas_call(
        functools.partial(_in_proj_body, seq // tm),
        grid=(nt,),
        in_specs=[
            pl.BlockSpec((tm, D_MODEL), row),
            pl.BlockSpec((SUBLANES, D_MODEL), lambda i: (jnp.maximum(i * r - 1, 0), 0)),
            pl.BlockSpec((SUBLANES, D_MODEL), lambda i: (jnp.minimum((i + 1) * r, last_blk), 0)),
            pl.BlockSpec((1, D_MODEL), const),
            pl.BlockSpec((D_MODEL, D_IN_PAD), const),
            pl.BlockSpec((CONV_W, D_LRU), const),
            pl.BlockSpec((1, D_LRU), const),
        ],
        out_specs=[pl.BlockSpec((tm, s.shape[1]), row) for s in out_shapes],
        out_shape=out_shapes,
        scratch_shapes=[pltpu.VMEM((tm + 2 * SUBLANES, D_LRU), f32)],
        compiler_params=_cparams(1),
        name="in_proj",
    )(x2, x2, x2, g, w, cw, cb)


def _lru_tile(u_ref, d, reverse, wg_ref, bg_ref, lam_ref, a_scr, b_scr, hcar_ref, h_out_ref):
    n = u_ref.shape[0]
    u = u_ref[...]
    gates = jnp.dot(u.astype(bf16), wg_ref[d], preferred_element_type=f32) + bg_ref[d]
    r = _sigmoid(gates[:, :D_LRU])
    ig = _sigmoid(gates[:, D_LRU:])
    log_a = r * (LRU_C * _log_sigmoid(lam_ref[d]))
    a = jnp.exp(log_a)
    b = jnp.sqrt(1.0 - a * a) * (ig * u)

    seg = n // SUBLANES
    chunks = range(D_LRU // LANES)
    for c in chunks:
        cols = slice(c * LANES, (c + 1) * LANES)
        for s in range(SUBLANES):
            a_scr[c, pl.ds(s, seg, stride=SUBLANES), :] = a[s * seg:(s + 1) * seg, cols]
            b_scr[c, pl.ds(s, seg, stride=SUBLANES), :] = b[s * seg:(s + 1) * seg, cols]

    acc = [jnp.ones((SUBLANES, LANES), f32) for _ in chunks]
    h = [jnp.zeros((SUBLANES, LANES), f32) for _ in chunks]
    for j in (range(seg - 1, -1, -1) if reverse else range(seg)):
        rows = pl.ds(j * SUBLANES, SUBLANES)
        for c in chunks:
            aj = a_scr[c, rows, :]
            h[c] = aj * h[c] + b_scr[c, rows, :]
            acc[c] = acc[c] * aj
            b_scr[c, rows, :] = h[c]
            a_scr[c, rows, :] = acc[c]

    row = lax.broadcasted_iota(i32, (SUBLANES, LANES), 0)
    h_in = []
    for c in chunks:
        a_tot, h_tot = acc[c], h[c]
        for s in (1, 2, 4):
            if reverse:
                keep = row < SUBLANES - s
                shift = SUBLANES - s
            else:
                keep = row >= s
                shift = s
            a_nb = jnp.where(keep, pltpu.roll(a_tot, shift, 0), 1.0)
            h_nb = jnp.where(keep, pltpu.roll(h_tot, shift, 0), 0.0)
            h_tot = a_tot * h_nb + h_tot
            a_tot = a_tot * a_nb
        carry = hcar_ref[d, c]
        h_edge = a_tot * carry + h_tot
        if reverse:
            h_in.append(jnp.where(row == SUBLANES - 1, carry,
                                  pltpu.roll(h_edge, SUBLANES - 1, 0)))
            edge = h_edge[0:1, :]
        else:
            h_in.append(jnp.where(row == 0, carry, pltpu.roll(h_edge, 1, 0)))
            edge = h_edge[SUBLANES - 1:SUBLANES, :]
        hcar_ref[d, c] = jnp.broadcast_to(edge, (SUBLANES, LANES))

    for j in range(seg):
        rows = pl.ds(j * SUBLANES, SUBLANES)
        for c in chunks:
            b_scr[c, rows, :] = b_scr[c, rows, :] + a_scr[c, rows, :] * h_in[c]
    for c in chunks:
        for s in range(SUBLANES):
            h_out_ref[s * seg:(s + 1) * seg, c * LANES:(c + 1) * LANES] = (
                b_scr[c, pl.ds(s, seg, stride=SUBLANES), :].astype(h_out_ref.dtype))


def _split_bf16(x):
    hi = x.astype(bf16)
    return hi, (x - hi.astype(f32)).astype(bf16)


def _state_diag_mask():
    return (lax.broadcasted_iota(i32, (HK, HV), 0) // GLA_DK
            == lax.broadcasted_iota(i32, (HK, HV), 1) // GLA_DV)


def _gla_fast(q, k, v, la, cum, tot, d, reverse, s_ref, o_out_ref):
    n = q.shape[0]
    ri = lax.broadcasted_iota(i32, (n, n), 0)
    ci = lax.broadcasted_iota(i32, (n, n), 1)
    causal = (ri < ci) if reverse else (ri >= ci)
    qa = q * jnp.exp(cum)
    kb = (k * jnp.exp(-cum)).astype(bf16)
    ke_t = (k * jnp.exp(tot - cum)).T.astype(bf16)

    state = s_ref[d]
    o_inter = jnp.dot(qa.astype(bf16), state.astype(bf16), preferred_element_type=f32)
    head_of_lane = lax.broadcasted_iota(i32, (1, HK), 1) // GLA_DK
    for h in range(GLA_HEADS):
        qh = jnp.where(head_of_lane == h, qa, 0.0).astype(bf16)
        s = lax.dot_general(qh, kb, (((1,), (1,)), ((), ())), preferred_element_type=f32)
        p = jnp.where(causal, s, 0.0).astype(bf16)
        lo, hi = h * GLA_DV, (h + 1) * GLA_DV
        o_h = jnp.dot(p, v[:, lo:hi], preferred_element_type=f32) + o_inter[:, lo:hi]
        o_out_ref[:, lo:hi] = o_h.astype(o_out_ref.dtype)

    kv = jnp.dot(ke_t, v, preferred_element_type=f32)
    dec = jnp.exp(jnp.sum(la.T, axis=1, keepdims=True))
    s_ref[d] = state * dec + jnp.where(_state_diag_mask(), kv, 0.0)


def _gla_safe(q, k, v, la, d, reverse, s_ref, o_out_ref):
    n = q.shape[0]
    c_len = GLA_SAFE_CHUNK
    n_chunks = n // c_len
    ri = lax.broadcasted_iota(i32, (n, n), 0)
    ci = lax.broadcasted_iota(i32, (n, n), 1)
    same = (ri // c_len) == (ci // c_len)
    incl = (ri <= ci) if reverse else (ri >= ci)
    la_hi, la_lo = _split_bf16(la)
    tri = jnp.where(same & incl, 1.0, 0.0).astype(bf16)
    ones = jnp.where(same, 1.0, 0.0).astype(bf16)
    lc = (jnp.dot(tri, la_hi, preferred_element_type=f32)
          + jnp.dot(tri, la_lo, preferred_element_type=f32))
    totc = (jnp.dot(ones, la_hi, preferred_element_type=f32)
            + jnp.dot(ones, la_lo, preferred_element_type=f32))
    vf = v.astype(f32)
    pos = lax.broadcasted_iota(i32, (n, 1), 0) % c_len
    diag = _state_diag_mask()
    head_sum = jnp.where(diag, 1.0, 0.0).astype(bf16)

    o = jnp.zeros((n, HV), f32)
    for delta in range(1 if reverse else 0, c_len):
        if delta == 0:
            k_j, lc_j, v_j = k, lc, vf
        else:
            shift = n - delta if reverse else delta
            k_j = pltpu.roll(k, shift, 0)
            lc_j = pltpu.roll(lc, shift, 0)
            v_j = pltpu.roll(vf, shift, 0)
        valid = (pos + delta < c_len) if reverse else (pos >= delta)
        dec = jnp.exp(jnp.where(valid, lc - lc_j, -1e30))
        scores = jnp.dot((q * k_j * dec).astype(bf16), head_sum, preferred_element_type=f32)
        o = o + scores * v_j

    ql = q * jnp.exp(lc)
    ke_t = (k * jnp.exp(totc - lc)).T.astype(bf16)
    la_t = la.T
    chunk_of_row = lax.broadcasted_iota(i32, (n, 1), 0) // c_len
    chunk_of_col = lax.broadcasted_iota(i32, (1, n), 1) // c_len

    def chunk(step, o_acc):
        c = (n_chunks - 1 - step) if reverse else step
        state = s_ref[d]
        rows = chunk_of_row == c
        qc = jnp.where(rows, ql, 0.0).astype(bf16)
        o_acc = o_acc + jnp.dot(qc, state.astype(bf16), preferred_element_type=f32)
        vc = jnp.where(rows, vf, 0.0).astype(bf16)
        kv = jnp.dot(ke_t, vc, preferred_element_type=f32)
        dec = jnp.exp(jnp.sum(jnp.where(chunk_of_col == c, la_t, 0.0), axis=1, keepdims=True))
        s_ref[d] = state * dec + jnp.where(diag, kv, 0.0)
        return o_acc

    o = lax.fori_loop(0, n_chunks, chunk, o)
    o_out_ref[...] = o.astype(o_out_ref.dtype)


def _gla_tile(q_ref, k_ref, v_ref, al_ref, d, reverse, wa_ref, ba_ref, s_ref, o_out_ref):
    n = q_ref.shape[0]
    z = jnp.dot(al_ref[...], wa_ref[d], preferred_element_type=f32,
                precision=lax.Precision.HIGHEST) + ba_ref[d]
    la = _log_sigmoid(z) * (1.0 / GLA_TAU)
    la_hi, la_lo = _split_bf16(la)
    ri = lax.broadcasted_iota(i32, (n, n), 0)
    ci = lax.broadcasted_iota(i32, (n, n), 1)
    tri = jnp.where((ri <= ci) if reverse else (ri >= ci), 1.0, 0.0).astype(bf16)
    cum = (jnp.dot(tri, la_hi, preferred_element_type=f32)
           + jnp.dot(tri, la_lo, preferred_element_type=f32))
    tot = cum[0:1, :] if reverse else cum[n - 1:n, :]
    q = q_ref[...].astype(f32) * (GLA_DK ** -0.5)
    k = k_ref[...].astype(f32)
    v = v_ref[...]
    strong_decay = jnp.min(tot) < -GLA_FAST_LIMIT

    @pl.when(jnp.logical_not(strong_decay))
    def _():
        _gla_fast(q, k, v, la, cum, tot, d, reverse, s_ref, o_out_ref)

    @pl.when(strong_decay)
    def _():
        _gla_safe(q, k, v, la, d, reverse, s_ref, o_out_ref)


def _mix_body(uf_ref, qf_ref, kf_ref, vf_ref, af_ref, ub_ref, qb_ref, kb_ref, vb_ref, ab_ref,
              wg_ref, bg_ref, lam_ref, wa_ref, ba_ref,
              hf_ref, hb_ref, of_ref, ob_ref,
              a_scr, b_scr, hcar_ref, s_ref):
    @pl.when(pl.program_id(1) == 0)
    def _():
        hcar_ref[...] = jnp.zeros_like(hcar_ref)
        s_ref[...] = jnp.zeros_like(s_ref)

    _lru_tile(uf_ref, 0, False, wg_ref, bg_ref, lam_ref, a_scr, b_scr, hcar_ref, hf_ref)
    _lru_tile(ub_ref, 1, True, wg_ref, bg_ref, lam_ref, a_scr, b_scr, hcar_ref, hb_ref)
    _gla_tile(qf_ref, kf_ref, vf_ref, af_ref, 0, False, wa_ref, ba_ref, s_ref, of_ref)
    _gla_tile(qb_ref, kb_ref, vb_ref, ab_ref, 1, True, wa_ref, ba_ref, s_ref, ob_ref)


def _mix(u, q, k, v, al, wg, bg, lam, wa, ba):
    b, s, _ = u.shape
    n = min(L_MIX, s)
    nt = s // n
    fwd = lambda bi, j: (bi, j, 0)
    bwd = lambda bi, j: (bi, nt - 1 - j, 0)
    const3 = lambda bi, j: (0, 0, 0)

    def tile_specs(imap):
        return [pl.BlockSpec((None, n, c), imap) for c in (D_LRU, HK, HK, HV, 2 * GLA_RANK)]

    out_shape = [jax.ShapeDtypeStruct((b, s, c), bf16) for c in (D_LRU, D_LRU, HV, HV)]
    return pl.pallas_call(
        _mix_body,
        grid=(b, nt),
        in_specs=tile_specs(fwd) + tile_specs(bwd) + [
            pl.BlockSpec(wg.shape, const3),
            pl.BlockSpec(bg.shape, const3),
            pl.BlockSpec(lam.shape, const3),
            pl.BlockSpec(wa.shape, const3),
            pl.BlockSpec(ba.shape, const3),
        ],
        out_specs=[
            pl.BlockSpec((None, n, D_LRU), fwd),
            pl.BlockSpec((None, n, D_LRU), bwd),
            pl.BlockSpec((None, n, HV), fwd),
            pl.BlockSpec((None, n, HV), bwd),
        ],
        out_shape=out_shape,
        scratch_shapes=[
            pltpu.VMEM((D_LRU // LANES, n, LANES), f32),
            pltpu.VMEM((D_LRU // LANES, n, LANES), f32),
            pltpu.VMEM((2, D_LRU // LANES, SUBLANES, LANES), f32),
            pltpu.VMEM((2, HK, HV), f32),
        ],
        compiler_params=_cparams(2),
        name="mix",
    )(u, q, k, v, al, u, q, k, v, al, wg, bg, lam, wa, ba)


def _out_proj_body(hf_ref, hb_ref, gate_ref, of_ref, ob_ref, g2_ref, x_ref,
                   wout_ref, gn_ref, fg_ref, wr_ref, br_ref,
                   x1_ref, hn_ref, ri_ref, rg_ref, cnt_ref, cnt_scr):
    i = pl.program_id(0)
    tm = x_ref.shape[0]

    @pl.when(i == 0)
    def _():
        cnt_scr[...] = jnp.zeros_like(cnt_scr)

    hs = hf_ref[...].astype(f32) + hb_ref[...].astype(f32)
    gt = gate_ref[...].astype(f32)
    gelu = 0.5 * gt * (1.0 + jnp.tanh(0.7978845608028654 * (gt + 0.044715 * (gt * gt * gt))))
    lru_out = (hs * gelu).astype(bf16)

    o = of_ref[...].astype(f32) + ob_ref[...].astype(f32)
    g2 = g2_ref[...].astype(f32)
    silu = g2 * _sigmoid(g2)
    cat = [lru_out]
    for h in range(GLA_HEADS):
        lo, hi = h * GLA_DV, (h + 1) * GLA_DV
        oh = o[:, lo:hi]
        on = oh * lax.rsqrt(jnp.mean(oh * oh, axis=-1, keepdims=True) + EPS)
        cat.append((on * gn_ref[:, lo:hi] * silu[:, lo:hi]).astype(bf16))
    mix = jnp.dot(jnp.concatenate(cat, axis=1), wout_ref[...], preferred_element_type=f32)
    x1 = x_ref[...] + mix
    x1_ref[...] = x1
    hn = _rms(x1, fg_ref[...])
    _store_slabs(hn_ref, hn)

    hn_hi, hn_lo = _split_bf16(hn)
    logits = (jnp.dot(hn_hi, wr_ref[0], preferred_element_type=f32)
              + jnp.dot(hn_hi, wr_ref[1], preferred_element_type=f32)
              + jnp.dot(hn_lo, wr_ref[0], preferred_element_type=f32)) + br_ref[...]
    lane = lax.broadcasted_iota(i32, (tm, LANES), 1)
    lane_f = lane.astype(f32)
    sel_idx, sel_val = [], []
    member = jnp.zeros((tm, LANES), f32)
    for _ in range(TOP_K):
        m = jnp.max(logits, axis=-1, keepdims=True)
        idx = jnp.min(jnp.where(logits == m, lane_f, float(LANES)), axis=-1, keepdims=True)
        hit = lane_f == idx
        member = jnp.where(hit, 1.0, member)
        logits = jnp.where(hit, -jnp.inf, logits)
        sel_idx.append(idx)
        sel_val.append(m)
    ex = [jnp.exp(mv - sel_val[0]) for mv in sel_val]
    inv = 1.0 / (ex[0] + ex[1] + ex[2] + ex[3])

    ri = lax.broadcasted_iota(i32, (tm, tm), 0)
    ci = lax.broadcasted_iota(i32, (tm, tm), 1)
    before = jnp.where(ri > ci, 1.0, 0.0).astype(bf16)
    prefix = jnp.dot(before, member.astype(bf16), preferred_element_type=f32) + cnt_scr[...]
    cnt_new = cnt_scr[...] + jnp.sum(member, axis=0, keepdims=True)
    cnt_scr[...] = cnt_new
    cnt_ref[...] = cnt_new.astype(i32)

    route_i = jnp.zeros((tm, LANES), f32)
    route_g = jnp.zeros((tm, LANES), f32)
    for kk in range(TOP_K):
        rank = jnp.sum(jnp.where(lane_f == sel_idx[kk], prefix, 0.0), axis=-1, keepdims=True)
        route_i = jnp.where(lane == kk, sel_idx[kk], route_i)
        route_i = jnp.where(lane == TOP_K + kk, rank, route_i)
        route_g = jnp.where(lane == kk, ex[kk] * inv, route_g)
    ri_ref[...] = route_i.astype(i32)
    rg_ref[...] = route_g


def _out_proj(hf, hb, gate, of, ob, g2, x2, wout, gn, fg, wr, br):
    t = x2.shape[0]
    tm = min(TM_PROJ, t)
    row = lambda i: (i, 0)
    const = lambda i: (0, 0)
    out_shape = [
        jax.ShapeDtypeStruct((t, D_MODEL), f32),
        jax.ShapeDtypeStruct((t * SLAB, LANES), f32),
        jax.ShapeDtypeStruct((t, LANES), i32),
        jax.ShapeDtypeStruct((t, LANES), f32),
        jax.ShapeDtypeStruct((1, LANES), i32),
    ]
    return pl.pallas_call(
        _out_proj_body,
        grid=(t // tm,),
        in_specs=[
            pl.BlockSpec((tm, D_LRU), row),
            pl.BlockSpec((tm, D_LRU), row),
            pl.BlockSpec((tm, D_LRU), row),
            pl.BlockSpec((tm, HV), row),
            pl.BlockSpec((tm, HV), row),
            pl.BlockSpec((tm, HV), row),
            pl.BlockSpec((tm, D_MODEL), row),
            pl.BlockSpec((D_MODEL, D_MODEL), const),
            pl.BlockSpec((1, HV), const),
            pl.BlockSpec((1, D_MODEL), const),
            pl.BlockSpec((2, D_MODEL, LANES), lambda i: (0, 0, 0)),
            pl.BlockSpec((1, LANES), const),
        ],
        out_specs=[
            pl.BlockSpec((tm, D_MODEL), row),
            pl.BlockSpec((tm * SLAB, LANES), row),
            pl.BlockSpec((tm, LANES), row),
            pl.BlockSpec((tm, LANES), row),
            pl.BlockSpec((1, LANES), const),
        ],
        out_shape=out_shape,
        scratch_shapes=[pltpu.VMEM((1, LANES), f32)],
        compiler_params=_cparams(1),
        name="out_proj",
    )(hf, hb, gate, of, ob, g2, x2, wout, gn, fg, wr, br)


def _slab_copy(src, src_sub, dst, dst_sub, sem):
    return pltpu.make_async_copy(src.at[pl.ds(src_sub, SLAB)], dst.at[pl.ds(dst_sub, SLAB)], sem)


def _experts_body(be_ref, nreal_ref, src_hbm, dst_hbm, hn_hbm, wg_ref, wu_ref, bgt_ref, bup_ref,
                  wo_ref, bo_ref, y4_hbm, src0, src1, dst0, dst1, xbuf0, xbuf1, ybuf0, ybuf1,
                  isem, dsem, gsem, ssem):
    n = pl.program_id(0)
    n_blocks = pl.num_programs(0)
    bm = xbuf0.shape[0] // SLAB
    spare_sub = y4_hbm.shape[0] - bm * SLAB

    def used(m):
        inside = (m >= 0) & (m < n_blocks)
        return jnp.where(inside, nreal_ref[jnp.clip(m, 0, n_blocks - 1)], 0) > 0

    def src_copy(m, smem, sem):
        return pltpu.make_async_copy(src_hbm.at[jnp.minimum(m, n_blocks - 1)], smem, sem)

    def dst_copy(m, smem, sem):
        return pltpu.make_async_copy(dst_hbm.at[jnp.maximum(m, 0)], smem, sem)

    def issue_gather(smem, xb, sem):
        for r in range(bm):
            _slab_copy(hn_hbm, pl.multiple_of(smem[r], SLAB), xb, r * SLAB, sem).start()

    def wait_gather(xb, sem):
        pltpu.make_async_copy(hn_hbm.at[pl.ds(0, bm * SLAB)], xb, sem).wait()

    def issue_scatter(smem, yb):
        for r in range(bm):
            _slab_copy(yb, r * SLAB, y4_hbm, pl.multiple_of(smem[r], SLAB), ssem).start(priority=1)

    def wait_scatter(yb):
        pltpu.make_async_copy(yb, y4_hbm.at[pl.ds(0, bm * SLAB)], ssem).wait()

    def compute(xb, yb):
        x = _load_slabs(xb).astype(bf16)
        gate = jnp.dot(x, wg_ref[...], preferred_element_type=f32) + bgt_ref[...]
        up = jnp.dot(x, wu_ref[...], preferred_element_type=f32) + bup_ref[...]
        gate = jnp.minimum(gate, SWIGLU_LIMIT)
        up = jnp.clip(up, -SWIGLU_LIMIT, SWIGLU_LIMIT)
        glu = gate * _sigmoid(SWIGLU_ALPHA * gate)
        act = ((up + 1.0) * glu).astype(bf16)
        _store_slabs(yb, jnp.dot(act, wo_ref[...], preferred_element_type=f32) + bo_ref[...])

    bufs = (dict(x_cur=xbuf0, y_cur=ybuf0, g_cur=gsem.at[0], x_alt=xbuf1, y_alt=ybuf1,
                 g_alt=gsem.at[1], src_cur=src0, dst_cur=dst0, src_alt=src1, dst_alt=dst1,
                 i_cur=isem.at[0], d_cur=dsem.at[0], i_alt=isem.at[1], d_alt=dsem.at[1]),
            dict(x_cur=xbuf1, y_cur=ybuf1, g_cur=gsem.at[1], x_alt=xbuf0, y_alt=ybuf0,
                 g_alt=gsem.at[0], src_cur=src1, dst_cur=dst1, src_alt=src0, dst_alt=dst0,
                 i_cur=isem.at[1], d_cur=dsem.at[1], i_alt=isem.at[0], d_alt=dsem.at[0]))

    def step(parity, has_prev):
        b = bufs[parity]
        src_copy(n + 1, b["src_alt"], b["i_alt"]).wait()
        issue_gather(b["src_alt"], b["x_alt"], b["g_alt"])
        if has_prev:
            dst_copy(n - 1, b["dst_alt"], b["d_alt"]).wait()
            issue_scatter(b["dst_alt"], b["y_alt"])
        src_copy(n + 2, b["src_cur"], b["i_cur"]).start()
        dst_copy(n, b["dst_cur"], b["d_cur"]).start()
        wait_gather(b["x_cur"], b["g_cur"])
        compute(b["x_cur"], b["y_cur"])
        if has_prev:
            wait_scatter(b["y_alt"])

    def drain(parity):
        b = bufs[parity]
        src_copy(n + 1, b["src_alt"], b["i_alt"]).wait()
        wait_gather(b["x_cur"], b["g_cur"])
        dst_copy(n - 1, b["dst_alt"], b["d_alt"]).wait()
        issue_scatter(b["dst_alt"], b["y_alt"])
        wait_scatter(b["y_alt"])
        fill = pltpu.make_async_copy(b["y_alt"], y4_hbm.at[pl.ds(spare_sub, bm * SLAB)], ssem)
        fill.start()
        fill.wait()

    @pl.when((n == 0) & used(0))
    def _():
        first = src_copy(0, src0, isem.at[0])
        first.start()
        first.wait()
        src_copy(1, src1, isem.at[1]).start()
        issue_gather(src0, xbuf0, gsem.at[0])
        step(0, False)

    for parity in range(2):
        @pl.when((n > 0) & (n % 2 == parity) & used(n))
        def _():
            step(parity, True)

        @pl.when((n % 2 == parity) & used(n - 1) & jnp.logical_not(used(n)))
        def _():
            drain(parity)


def _experts(block_e, n_real, src_idx, dst_idx, hn, wg, wu, bgt, bup, wo, bo, n_out):
    n_blocks, bm = src_idx.shape
    wmap = lambda n, be, nu: (be[n], 0, 0)
    grid_spec = pltpu.PrefetchScalarGridSpec(
        num_scalar_prefetch=2,
        grid=(n_blocks,),
        in_specs=[
            pl.BlockSpec(memory_space=pl.ANY),
            pl.BlockSpec(memory_space=pl.ANY),
            pl.BlockSpec(memory_space=pl.ANY),
            pl.BlockSpec((None, D_MODEL, D_FF), wmap),
            pl.BlockSpec((None, D_MODEL, D_FF), wmap),
            pl.BlockSpec((None, 1, D_FF), wmap),
            pl.BlockSpec((None, 1, D_FF), wmap),
            pl.BlockSpec((None, D_FF, D_MODEL), wmap),
            pl.BlockSpec((None, 1, D_MODEL), wmap),
        ],
        out_specs=pl.BlockSpec(memory_space=pl.ANY),
        scratch_shapes=(
            [pltpu.SMEM((bm,), i32)] * 4
            + [pltpu.VMEM((bm * SLAB, LANES), f32)] * 4
            + [pltpu.SemaphoreType.DMA((2,))] * 3
            + [pltpu.SemaphoreType.DMA]),
    )
    return pl.pallas_call(
        _experts_body,
        grid_spec=grid_spec,
        out_shape=jax.ShapeDtypeStruct(((n_out + bm) * SLAB, LANES), f32),
        compiler_params=_cparams(1),
        name="experts",
    )(block_e, n_real, src_idx, dst_idx, hn, wg, wu, bgt, bup, wo, bo)


def _combine_body(y0_ref, y1_ref, y2_ref, y3_ref, rg_ref, x1_ref, g_ref, out_ref):
    acc = x1_ref[...]
    for kk, y_ref in enumerate((y0_ref, y1_ref, y2_ref, y3_ref)):
        acc = acc + rg_ref[:, kk:kk + 1] * _load_slabs(y_ref)
    out_ref[...] = _rms(acc, g_ref[...])


def _combine(y4, rg, x1, g):
    t = x1.shape[0]
    tm = min(TM_PROJ, t)
    row = lambda i: (i, 0)
    slot_specs = [pl.BlockSpec((tm * SLAB, LANES),
                               functools.partial(lambda kk, i: (kk * (t // tm) + i, 0), kk))
                  for kk in range(TOP_K)]
    return pl.pallas_call(
        _combine_body,
        grid=(t // tm,),
        in_specs=slot_specs + [
            pl.BlockSpec((tm, LANES), row),
            pl.BlockSpec((tm, D_MODEL), row),
            pl.BlockSpec((1, D_MODEL), lambda i: (0, 0)),
        ],
        out_specs=pl.BlockSpec((tm, D_MODEL), row),
        out_shape=jax.ShapeDtypeStruct((t, D_MODEL), f32),
        compiler_params=_cparams(1),
        name="combine",
    )(y4, y4, y4, y4, rg, x1, g)


REPACK_ROWS = 512
MXU_COLS = 256


def _repack_body(w_ref, gate_ref, up_ref):
    ci = lax.broadcasted_iota(i32, (MXU_COLS, MXU_COLS), 0)
    ji = lax.broadcasted_iota(i32, (MXU_COLS, MXU_COLS), 1)
    src = jnp.where(ji < LANES, 2 * ji, 2 * (ji - LANES) + 1)
    perm = jnp.where(ci == src, 1.0, 0.0).astype(bf16)
    for grp in range(w_ref.shape[1] // MXU_COLS):
        blk = w_ref[:, grp * MXU_COLS:(grp + 1) * MXU_COLS].astype(bf16)
        r = jnp.dot(blk, perm, preferred_element_type=f32)
        gate_ref[:, grp * LANES:(grp + 1) * LANES] = r[:, :LANES].astype(bf16)
        up_ref[:, grp * LANES:(grp + 1) * LANES] = r[:, LANES:].astype(bf16)


def _repack_expert_in(w):
    e, d, two_f = w.shape
    imap = lambda ei, ri: (ei, ri, 0)
    out = jax.ShapeDtypeStruct((e, d, two_f // 2), bf16)
    return pl.pallas_call(
        _repack_body,
        grid=(e, d // REPACK_ROWS),
        in_specs=[pl.BlockSpec((None, REPACK_ROWS, two_f), imap)],
        out_specs=[pl.BlockSpec((None, REPACK_ROWS, two_f // 2), imap)] * 2,
        out_shape=[out, out],
        compiler_params=_cparams(2),
        name="repack",
    )(w)


def _block_diag_dense(w):
    eye = jnp.eye(LRU_BLOCKS, dtype=w.dtype)
    return jnp.einsum('hij,hg->higj', w, eye).reshape(D_LRU, D_LRU)


def _prep(mix_norm_g, w_mix_in, lru_conv_w, lru_conv_b, lru_w_r, lru_b_r, lru_w_i, lru_b_i,
          lru_lambda, gla_w_alpha, gla_b_alpha, gla_norm_g, w_mix_out, ffn_norm_g,
          w_router, b_router, w_exp_in, b_exp_in, w_exp_out, b_exp_out, final_norm_g):
    p = {}
    p["mix_g"] = mix_norm_g[0].reshape(1, D_MODEL)
    p["w_in"] = jnp.pad(w_mix_in[0], ((0, 0), (0, D_IN_PAD - w_mix_in.shape[-1]))).astype(bf16)
    p["conv_w"] = lru_conv_w[0]
    p["conv_b"] = lru_conv_b[0].reshape(1, D_LRU)
    p["wg"] = jnp.stack([
        jnp.concatenate([_block_diag_dense(lru_w_r[0, d]), _block_diag_dense(lru_w_i[0, d])], axis=1)
        for d in range(2)]).astype(bf16)
    p["bg"] = jnp.concatenate([lru_b_r[0], lru_b_i[0]], axis=-1).reshape(2, 1, 2 * D_LRU)
    p["lam"] = lru_lambda[0].reshape(2, 1, D_LRU)
    zeros = jnp.zeros((GLA_RANK, HK), f32)
    p["wa"] = jnp.stack([jnp.concatenate([gla_w_alpha[0, 0], zeros], axis=0),
                         jnp.concatenate([zeros, gla_w_alpha[0, 1]], axis=0)])
    p["ba"] = gla_b_alpha[0].reshape(2, 1, HK)
    p["gn"] = gla_norm_g[0].reshape(1, HV)
    p["w_out"] = w_mix_out[0].astype(bf16)
    p["ffn_g"] = ffn_norm_g[0].reshape(1, D_MODEL)
    w_r = jnp.pad(w_router[0], ((0, 0), (0, LANES - N_EXPERTS)))
    w_r_hi = w_r.astype(bf16)
    p["w_r"] = jnp.stack([w_r_hi, (w_r - w_r_hi.astype(f32)).astype(bf16)])
    p["b_r"] = jnp.pad(b_router[0], (0, LANES - N_EXPERTS), constant_values=-1e30).reshape(1, LANES)
    p["w_gate"], p["w_up"] = _repack_expert_in(w_exp_in[0])
    p["b_gate"] = b_exp_in[0, :, 0::2].reshape(N_EXPERTS, 1, D_FF)
    p["b_up"] = b_exp_in[0, :, 1::2].reshape(N_EXPERTS, 1, D_FF)
    p["w_eo"] = w_exp_out[0].astype(bf16)
    p["b_eo"] = b_exp_out[0].reshape(N_EXPERTS, 1, D_MODEL)
    p["final_g"] = final_norm_g.reshape(1, D_MODEL)
    return p


def _route_tables(route_i, counts, t):
    bm = BM_EXP
    n_assign = t * TOP_K
    n_blocks = n_assign // bm + N_EXPERTS
    cap = n_blocks * bm
    cnt = counts[0, :N_EXPERTS]
    padded = ((cnt + bm - 1) // bm) * bm
    pad_ends = jnp.cumsum(padded)
    pad_starts = pad_ends - padded
    e = route_i[:, :TOP_K]
    rank = route_i[:, TOP_K:2 * TOP_K]
    pos = (pad_starts[e] + rank).reshape(-1)
    slot_of_row = jnp.full((cap,), -1, i32).at[pos].set(
        jnp.arange(n_assign, dtype=i32), unique_indices=True)
    real = slot_of_row >= 0
    src_tok = jnp.where(real, slot_of_row // TOP_K, 0)
    spare = n_assign + jnp.arange(cap, dtype=i32) % bm
    dst_row = jnp.where(real, (slot_of_row % TOP_K) * t + slot_of_row // TOP_K, spare)
    src_idx = src_tok.reshape(n_blocks, bm) * SLAB
    dst_idx = dst_row.reshape(n_blocks, bm) * SLAB
    block_start = jnp.arange(n_blocks, dtype=i32) * bm
    block_e = jnp.minimum(jnp.sum(pad_ends[None, :] <= block_start[:, None], axis=1),
                          N_EXPERTS - 1).astype(i32)
    n_real = jnp.clip(cnt[block_e] - (block_start - pad_starts[block_e]), 0, bm).astype(i32)
    return block_e, n_real, src_idx, dst_idx


def _trunk(x, p):
    b, s, _ = x.shape
    t = b * s
    x2 = x.reshape(t, D_MODEL)
    u, gate, q, k, v, g2, al = _in_proj(x2, s, p["mix_g"], p["w_in"], p["conv_w"], p["conv_b"])
    r3 = lambda a: a.reshape(b, s, a.shape[-1])
    hf, hb, of, ob = _mix(r3(u), r3(q), r3(k), r3(v), r3(al),
                          p["wg"], p["bg"], p["lam"], p["wa"], p["ba"])
    f2 = lambda a: a.reshape(t, a.shape[-1])
    x1, hn, route_i, route_g, counts = _out_proj(
        f2(hf), f2(hb), gate, f2(of), f2(ob), g2, x2,
        p["w_out"], p["gn"], p["ffn_g"], p["w_r"], p["b_r"])
    block_e, n_real, src_idx, dst_idx = _route_tables(route_i, counts, t)
    y4 = _experts(block_e, n_real, src_idx, dst_idx, hn, p["w_gate"], p["w_up"], p["b_gate"], p["b_up"],
                  p["w_eo"], p["b_eo"], t * TOP_K)
    y = _combine(y4, route_g, x1, p["final_g"])
    return y.reshape(b, s, D_MODEL)


def kernel(x_prompt, x_sample, mix_norm_g, w_mix_in, lru_conv_w, lru_conv_b, lru_w_r, lru_b_r,
           lru_w_i, lru_b_i, lru_lambda, gla_w_alpha, gla_b_alpha, gla_norm_g, w_mix_out,
           ffn_norm_g, w_router, b_router, w_exp_in, b_exp_in, w_exp_out, b_exp_out,
           final_norm_g):
    p = _prep(mix_norm_g, w_mix_in, lru_conv_w, lru_conv_b, lru_w_r, lru_b_r, lru_w_i, lru_b_i,
              lru_lambda, gla_w_alpha, gla_b_alpha, gla_norm_g, w_mix_out, ffn_norm_g,
              w_router, b_router, w_exp_in, b_exp_in, w_exp_out, b_exp_out, final_norm_g)
    return (_trunk(x_prompt, p), _trunk(x_sample, p))
```

```python
import functools

import jax
import jax.numpy as jnp
from jax import lax
from jax.experimental import pallas as pl
from jax.experimental.pallas import tpu as pltpu

f32 = jnp.float32
bf16 = jnp.bfloat16
i32 = jnp.int32

D_MODEL = 1024
D_LRU = 512
LRU_BLOCKS = 8
LRU_BW = 64
CONV_W = 4
LRU_C = 8.0
GLA_HEADS = 4
GLA_DK = 64
GLA_DV = 128
HK = GLA_HEADS * GLA_DK
HV = GLA_HEADS * GLA_DV
GLA_RANK = 16
GLA_TAU = 16.0
GLA_FAST_LIMIT = 60.0
GLA_SAFE_CHUNK = 16
N_EXPERTS = 32
TOP_K = 4
D_FF = 1024
SWIGLU_LIMIT = 7.0
SWIGLU_ALPHA = 1.702
EPS = 1e-6

LANES = 128
SUBLANES = 8
D_IN_PAD = 2688
VMEM_LIMIT = 56 * 1024 * 1024

TM_PROJ = 512
L_MIX = 256
BM_EXP = 256


def _cparams(n_axes):
    return pltpu.CompilerParams(
        dimension_semantics=("arbitrary",) * n_axes, vmem_limit_bytes=VMEM_LIMIT)


def _rms(x, g):
    ms = jnp.mean(x * x, axis=-1, keepdims=True)
    return x * lax.rsqrt(ms + EPS) * g


def _sigmoid(x):
    return 1.0 / (1.0 + jnp.exp(-x))


SLAB = D_MODEL // LANES


def _store_slabs(ref, x):
    n = x.shape[0]
    for s in range(SLAB):
        ref[pl.ds(s, n, stride=SLAB), :] = x[:, s * LANES:(s + 1) * LANES]


def _load_slabs(ref):
    n = ref.shape[0] // SLAB
    return jnp.concatenate([ref[pl.ds(s, n, stride=SLAB), :] for s in range(SLAB)], axis=1)


def _log_sigmoid(x):
    return jnp.minimum(x, 0.0) - jnp.log1p(jnp.exp(-jnp.abs(x)))


def _in_proj_body(tiles_per_seq, x_ref, xprev_ref, xnext_ref, g_ref, w_ref, cw_ref, cb_ref,
                  u_ref, gate_ref, q_ref, k_ref, v_ref, g2_ref, a_ref, ext_ref):
    i = pl.program_id(0)
    tm = x_ref.shape[0]
    g = g_ref[...]
    xn = _rms(x_ref[...], g).astype(bf16)

    def proj(lo, hi):
        return jnp.dot(xn, w_ref[:, lo:hi], preferred_element_type=f32)

    gate_ref[...] = proj(512, 1024).astype(bf16)
    q_ref[...] = proj(1024, 1280).astype(bf16)
    k_ref[...] = proj(1280, 1536).astype(bf16)
    v_ref[...] = proj(1536, 2048).astype(bf16)
    g2_ref[...] = proj(2048, 2560).astype(bf16)
    a_ref[...] = proj(2560, 2688)[:, :2 * GLA_RANK]

    xh = jnp.concatenate([xprev_ref[...], xnext_ref[...]], axis=0)
    hl = jnp.dot(_rms(xh, g).astype(bf16), w_ref[:, 0:512], preferred_element_type=f32)
    pos = i % tiles_per_seq
    ext_ref[0:SUBLANES, :] = jnp.where(pos == 0, 0.0, hl[:SUBLANES])
    ext_ref[SUBLANES:SUBLANES + tm, :] = proj(0, 512)
    ext_ref[SUBLANES + tm:, :] = jnp.where(pos == tiles_per_seq - 1, 0.0, hl[SUBLANES:])
    u = cb_ref[...]
    for j in range(CONV_W):
        u = u + cw_ref[j:j + 1, :] * ext_ref[pl.ds(SUBLANES - CONV_W // 2 + j, tm), :]
    u_ref[...] = u


def _in_proj(x2, seq, g, w, cw, cb):
    t = x2.shape[0]
    tm = min(TM_PROJ, seq)
    nt = t // tm
    r = tm // SUBLANES
    last_blk = t // SUBLANES - 1
    row = lambda i: (i, 0)
    const = lambda i: (0, 0)
    out_shapes = [
        jax.ShapeDtypeStruct((t, D_LRU), f32),
        jax.ShapeDtypeStruct((t, D_LRU), bf16),
        jax.ShapeDtypeStruct((t, HK), bf16),
        jax.ShapeDtypeStruct((t, HK), bf16),
        jax.ShapeDtypeStruct((t, HV), bf16),
        jax.ShapeDtypeStruct((t, HV), bf16),
        jax.ShapeDtypeStruct((t, 2 * GLA_RANK), f32),
    ]
    return pl.pallas_call(
        functools.partial(_in_proj_body, seq // tm),
        grid=(nt,),
        in_specs=[
            pl.BlockSpec((tm, D_MODEL), row),
            pl.BlockSpec((SUBLANES, D_MODEL), lambda i: (jnp.maximum(i * r - 1, 0), 0)),
            pl.BlockSpec((SUBLANES, D_MODEL), lambda i: (jnp.minimum((i + 1) * r, last_blk), 0)),
            pl.BlockSpec((1, D_MODEL), const),
            pl.BlockSpec((D_MODEL, D_IN_PAD), const),
            pl.BlockSpec((CONV_W, D_LRU), const),
            pl.BlockSpec((1, D_LRU), const),
        ],
        out_specs=[pl.BlockSpec((tm, s.shape[1]), row) for s in out_shapes],
        out_shape=out_shapes,
        scratch_shapes=[pltpu.VMEM((tm + 2 * SUBLANES, D_LRU), f32)],
        compiler_params=_cparams(1),
        name="in_proj",
    )(x2, x2, x2, g, w, cw, cb)


def _lru_tile(u_ref, d, reverse, wg_ref, bg_ref, lam_ref, a_scr, b_scr, hcar_ref, h_out_ref):
    n = u_ref.shape[0]
    u = u_ref[...]
    gates = jnp.dot(u.astype(bf16), wg_ref[d], preferred_element_type=f32) + bg_ref[d]
    r = _sigmoid(gates[:, :D_LRU])
    ig = _sigmoid(gates[:, D_LRU:])
    log_a = r * (LRU_C * _log_sigmoid(lam_ref[d]))
    a = jnp.exp(log_a)
    b = jnp.sqrt(1.0 - a * a) * (ig * u)

    seg = n // SUBLANES
    chunks = range(D_LRU // LANES)
    for c in chunks:
        cols = slice(c * LANES, (c + 1) * LANES)
        for s in range(SUBLANES):
            a_scr[c, pl.ds(s, seg, stride=SUBLANES), :] = a[s * seg:(s + 1) * seg, cols]
            b_scr[c, pl.ds(s, seg, stride=SUBLANES), :] = b[s * seg:(s + 1) * seg, cols]

    acc = [jnp.ones((SUBLANES, LANES), f32) for _ in chunks]
    h = [jnp.zeros((SUBLANES, LANES), f32) for _ in chunks]
    for j in (range(seg - 1, -1, -1) if reverse else range(seg)):
        rows = pl.ds(j * SUBLANES, SUBLANES)
        for c in chunks:
            aj = a_scr[c, rows, :]
            h[c] = aj * h[c] + b_scr[c, rows, :]
            acc[c] = acc[c] * aj
            b_scr[c, rows, :] = h[c]
            a_scr[c, rows, :] = acc[c]

    row = lax.broadcasted_iota(i32, (SUBLANES, LANES), 0)
    h_in = []
    for c in chunks:
        a_tot, h_tot = acc[c], h[c]
        for s in (1, 2, 4):
            if reverse:
                keep = row < SUBLANES - s
                shift = SUBLANES - s
            else:
                keep = row >= s
                shift = s
            a_nb = jnp.where(keep, pltpu.roll(a_tot, shift, 0), 1.0)
            h_nb = jnp.where(keep, pltpu.roll(h_tot, shift, 0), 0.0)
            h_tot = a_tot * h_nb + h_tot
            a_tot = a_tot * a_nb
        carry = hcar_ref[d, c]
        h_edge = a_tot * carry + h_tot
        if reverse:
            h_in.append(jnp.where(row == SUBLANES - 1, carry,
                                  pltpu.roll(h_edge, SUBLANES - 1, 0)))
            edge = h_edge[0:1, :]
        else:
            h_in.append(jnp.where(row == 0, carry, pltpu.roll(h_edge, 1, 0)))
            edge = h_edge[SUBLANES - 1:SUBLANES, :]
        hcar_ref[d, c] = jnp.broadcast_to(edge, (SUBLANES, LANES))

    for j in range(seg):
        rows = pl.ds(j * SUBLANES, SUBLANES)
        for c in chunks:
            b_scr[c, rows, :] = b_scr[c, rows, :] + a_scr[c, rows, :] * h_in[c]
    for c in chunks:
        for s in range(SUBLANES):
            h_out_ref[s * seg:(s + 1) * seg, c * LANES:(c + 1) * LANES] = (
                b_scr[c, pl.ds(s, seg, stride=SUBLANES), :].astype(h_out_ref.dtype))


def _split_bf16(x):
    hi = x.astype(bf16)
    return hi, (x - hi.astype(f32)).astype(bf16)


def _state_diag_mask():
    return (lax.broadcasted_iota(i32, (HK, HV), 0) // GLA_DK
            == lax.broadcasted_iota(i32, (HK, HV), 1) // GLA_DV)


def _gla_fast(q, k, v, la, cum, tot, d, reverse, s_ref, o_out_ref):
    n = q.shape[0]
    ri = lax.broadcasted_iota(i32, (n, n), 0)
    ci = lax.broadcasted_iota(i32, (n, n), 1)
    causal = (ri < ci) if reverse else (ri >= ci)
    qa = q * jnp.exp(cum)
    kb = (k * jnp.exp(-cum)).astype(bf16)
    ke_t = (k * jnp.exp(tot - cum)).T.astype(bf16)

    state = s_ref[d]
    o_inter = jnp.dot(qa.astype(bf16), state.astype(bf16), preferred_element_type=f32)
    head_of_lane = lax.broadcasted_iota(i32, (1, HK), 1) // GLA_DK
    for h in range(GLA_HEADS):
        qh = jnp.where(head_of_lane == h, qa, 0.0).astype(bf16)
        s = lax.dot_general(qh, kb, (((1,), (1,)), ((), ())), preferred_element_type=f32)
        p = jnp.where(causal, s, 0.0).astype(bf16)
        lo, hi = h * GLA_DV, (h + 1) * GLA_DV
        o_h = jnp.dot(p, v[:, lo:hi], preferred_element_type=f32) + o_inter[:, lo:hi]
        o_out_ref[:, lo:hi] = o_h.astype(o_out_ref.dtype)

    kv = jnp.dot(ke_t, v, preferred_element_type=f32)
    dec = jnp.exp(jnp.sum(la.T, axis=1, keepdims=True))
    s_ref[d] = state * dec + jnp.where(_state_diag_mask(), kv, 0.0)


def _gla_safe(q, k, v, la, d, reverse, s_ref, o_out_ref):
    n = q.shape[0]
    c_len = GLA_SAFE_CHUNK
    n_chunks = n // c_len
    ri = lax.broadcasted_iota(i32, (n, n), 0)
    ci = lax.broadcasted_iota(i32, (n, n), 1)
    same = (ri // c_len) == (ci // c_len)
    incl = (ri <= ci) if reverse else (ri >= ci)
    la_hi, la_lo = _split_bf16(la)
    tri = jnp.where(same & incl, 1.0, 0.0).astype(bf16)
    ones = jnp.where(same, 1.0, 0.0).astype(bf16)
    lc = (jnp.dot(tri, la_hi, preferred_element_type=f32)
          + jnp.dot(tri, la_lo, preferred_element_type=f32))
    totc = (jnp.dot(ones, la_hi, preferred_element_type=f32)
            + jnp.dot(ones, la_lo, preferred_element_type=f32))
    vf = v.astype(f32)
    pos = lax.broadcasted_iota(i32, (n, 1), 0) % c_len
    diag = _state_diag_mask()
    head_sum = jnp.where(diag, 1.0, 0.0).astype(bf16)

    o = jnp.zeros((n, HV), f32)
    for delta in range(1 if reverse else 0, c_len):
        if delta == 0:
            k_j, lc_j, v_j = k, lc, vf
        else:
            shift = n - delta if reverse else delta
            k_j = pltpu.roll(k, shift, 0)
            lc_j = pltpu.roll(lc, shift, 0)
            v_j = pltpu.roll(vf, shift, 0)
        valid = (pos + delta < c_len) if reverse else (pos >= delta)
        dec = jnp.exp(jnp.where(valid, lc - lc_j, -1e30))
        scores = jnp.dot((q * k_j * dec).astype(bf16), head_sum, preferred_element_type=f32)
        o = o + scores * v_j

    ql = q * jnp.exp(lc)
    ke_t = (k * jnp.exp(totc - lc)).T.astype(bf16)
    la_t = la.T
    chunk_of_row = lax.broadcasted_iota(i32, (n, 1), 0) // c_len
    chunk_of_col = lax.broadcasted_iota(i32, (1, n), 1) // c_len

    def chunk(step, o_acc):
        c = (n_chunks - 1 - step) if reverse else step
        state = s_ref[d]
        rows = chunk_of_row == c
        qc = jnp.where(rows, ql, 0.0).astype(bf16)
        o_acc = o_acc + jnp.dot(qc, state.astype(bf16), preferred_element_type=f32)
        vc = jnp.where(rows, vf, 0.0).astype(bf16)
        kv = jnp.dot(ke_t, vc, preferred_element_type=f32)
        dec = jnp.exp(jnp.sum(jnp.where(chunk_of_col == c, la_t, 0.0), axis=1, keepdims=True))
        s_ref[d] = state * dec + jnp.where(diag, kv, 0.0)
        return o_acc

    o = lax.fori_loop(0, n_chunks, chunk, o)
    o_out_ref[...] = o.astype(o_out_ref.dtype)


def _gla_tile(q_ref, k_ref, v_ref, al_ref, d, reverse, wa_ref, ba_ref, s_ref, o_out_ref):
    n = q_ref.shape[0]
    z = jnp.dot(al_ref[...], wa_ref[d], preferred_element_type=f32,
                precision=lax.Precision.HIGHEST) + ba_ref[d]
    la = _log_sigmoid(z) * (1.0 / GLA_TAU)
    la_hi, la_lo = _split_bf16(la)
    ri = lax.broadcasted_iota(i32, (n, n), 0)
    ci = lax.broadcasted_iota(i32, (n, n), 1)
    tri = jnp.where((ri <= ci) if reverse else (ri >= ci), 1.0, 0.0).astype(bf16)
    cum = (jnp.dot(tri, la_hi, preferred_element_type=f32)
           + jnp.dot(tri, la_lo, preferred_element_type=f32))
    tot = cum[0:1, :] if reverse else cum[n - 1:n, :]
    q = q_ref[...].astype(f32) * (GLA_DK ** -0.5)
    k = k_ref[...].astype(f32)
    v = v_ref[...]
    strong_decay = jnp.min(tot) < -GLA_FAST_LIMIT

    @pl.when(jnp.logical_not(strong_decay))
    def _():
        _gla_fast(q, k, v, la, cum, tot, d, reverse, s_ref, o_out_ref)

    @pl.when(strong_decay)
    def _():
        _gla_safe(q, k, v, la, d, reverse, s_ref, o_out_ref)


def _mix_body(uf_ref, qf_ref, kf_ref, vf_ref, af_ref, ub_ref, qb_ref, kb_ref, vb_ref, ab_ref,
              wg_ref, bg_ref, lam_ref, wa_ref, ba_ref,
              hf_ref, hb_ref, of_ref, ob_ref,
              a_scr, b_scr, hcar_ref, s_ref):
    @pl.when(pl.program_id(1) == 0)
    def _():
        hcar_ref[...] = jnp.zeros_like(hcar_ref)
        s_ref[...] = jnp.zeros_like(s_ref)

    _lru_tile(uf_ref, 0, False, wg_ref, bg_ref, lam_ref, a_scr, b_scr, hcar_ref, hf_ref)
    _lru_tile(ub_ref, 1, True, wg_ref, bg_ref, lam_ref, a_scr, b_scr, hcar_ref, hb_ref)
    _gla_tile(qf_ref, kf_ref, vf_ref, af_ref, 0, False, wa_ref, ba_ref, s_ref, of_ref)
    _gla_tile(qb_ref, kb_ref, vb_ref, ab_ref, 1, True, wa_ref, ba_ref, s_ref, ob_ref)


def _mix(u, q, k, v, al, wg, bg, lam, wa, ba):
    b, s, _ = u.shape
    n = min(L_MIX, s)
    nt = s // n
    fwd = lambda bi, j: (bi, j, 0)
    bwd = lambda bi, j: (bi, nt - 1 - j, 0)
    const3 = lambda bi, j: (0, 0, 0)

    def tile_specs(imap):
        return [pl.BlockSpec((None, n, c), imap) for c in (D_LRU, HK, HK, HV, 2 * GLA_RANK)]

    out_shape = [jax.ShapeDtypeStruct((b, s, c), bf16) for c in (D_LRU, D_LRU, HV, HV)]
    return pl.pallas_call(
        _mix_body,
        grid=(b, nt),
        in_specs=tile_specs(fwd) + tile_specs(bwd) + [
            pl.BlockSpec(wg.shape, const3),
            pl.BlockSpec(bg.shape, const3),
            pl.BlockSpec(lam.shape, const3),
            pl.BlockSpec(wa.shape, const3),
            pl.BlockSpec(ba.shape, const3),
        ],
        out_specs=[
            pl.BlockSpec((None, n, D_LRU), fwd),
            pl.BlockSpec((None, n, D_LRU), bwd),
            pl.BlockSpec((None, n, HV), fwd),
            pl.BlockSpec((None, n, HV), bwd),
        ],
        out_shape=out_shape,
        scratch_shapes=[
            pltpu.VMEM((D_LRU // LANES, n, LANES), f32),
            pltpu.VMEM((D_LRU // LANES, n, LANES), f32),
            pltpu.VMEM((2, D_LRU // LANES, SUBLANES, LANES), f32),
            pltpu.VMEM((2, HK, HV), f32),
        ],
        compiler_params=_cparams(2),
        name="mix",
    )(u, q, k, v, al, u, q, k, v, al, wg, bg, lam, wa, ba)


def _out_proj_body(hf_ref, hb_ref, gate_ref, of_ref, ob_ref, g2_ref, x_ref,
                   wout_ref, gn_ref, fg_ref, wr_ref, br_ref,
                   x1_ref, hn_ref, ri_ref, rg_ref, cnt_ref, cnt_scr):
    i = pl.program_id(0)
    tm = x_ref.shape[0]

    @pl.when(i == 0)
    def _():
        cnt_scr[...] = jnp.zeros_like(cnt_scr)

    hs = hf_ref[...].astype(f32) + hb_ref[...].astype(f32)
    gt = gate_ref[...].astype(f32)
    gelu = 0.5 * gt * (1.0 + jnp.tanh(0.7978845608028654 * (gt + 0.044715 * (gt * gt * gt))))
    lru_out = (hs * gelu).astype(bf16)

    o = of_ref[...].astype(f32) + ob_ref[...].astype(f32)
    g2 = g2_ref[...].astype(f32)
    silu = g2 * _sigmoid(g2)
    cat = [lru_out]
    for h in range(GLA_HEADS):
        lo, hi = h * GLA_DV, (h + 1) * GLA_DV
        oh = o[:, lo:hi]
        on = oh * lax.rsqrt(jnp.mean(oh * oh, axis=-1, keepdims=True) + EPS)
        cat.append((on * gn_ref[:, lo:hi] * silu[:, lo:hi]).astype(bf16))
    mix = jnp.dot(jnp.concatenate(cat, axis=1), wout_ref[...], preferred_element_type=f32)
    x1 = x_ref[...] + mix
    x1_ref[...] = x1
    hn = _rms(x1, fg_ref[...])
    _store_slabs(hn_ref, hn)

    hn_hi, hn_lo = _split_bf16(hn)
    logits = (jnp.dot(hn_hi, wr_ref[0], preferred_element_type=f32)
              + jnp.dot(hn_hi, wr_ref[1], preferred_element_type=f32)
              + jnp.dot(hn_lo, wr_ref[0], preferred_element_type=f32)) + br_ref[...]
    lane = lax.broadcasted_iota(i32, (tm, LANES), 1)
    lane_f = lane.astype(f32)
    sel_idx, sel_val = [], []
    member = jnp.zeros((tm, LANES), f32)
    for _ in range(TOP_K):
        m = jnp.max(logits, axis=-1, keepdims=True)
        idx = jnp.min(jnp.where(logits == m, lane_f, float(LANES)), axis=-1, keepdims=True)
        hit = lane_f == idx
        member = jnp.where(hit, 1.0, member)
        logits = jnp.where(hit, -jnp.inf, logits)
        sel_idx.append(idx)
        sel_val.append(m)
    ex = [jnp.exp(mv - sel_val[0]) for mv in sel_val]
    inv = 1.0 / (ex[0] + ex[1] + ex[2] + ex[3])

    ri = lax.broadcasted_iota(i32, (tm, tm), 0)
    ci = lax.broadcasted_iota(i32, (tm, tm), 1)
    before = jnp.where(ri > ci, 1.0, 0.0).astype(bf16)
    prefix = jnp.dot(before, member.astype(bf16), preferred_element_type=f32) + cnt_scr[...]
    cnt_new = cnt_scr[...] + jnp.sum(member, axis=0, keepdims=True)
    cnt_scr[...] = cnt_new
    cnt_ref[...] = cnt_new.astype(i32)

    route_i = jnp.zeros((tm, LANES), f32)
    route_g = jnp.zeros((tm, LANES), f32)
    for kk in range(TOP_K):
        rank = jnp.sum(jnp.where(lane_f == sel_idx[kk], prefix, 0.0), axis=-1, keepdims=True)
        route_i = jnp.where(lane == kk, sel_idx[kk], route_i)
        route_i = jnp.where(lane == TOP_K + kk, rank, route_i)
        route_g = jnp.where(lane == kk, ex[kk] * inv, route_g)
    ri_ref[...] = route_i.T[:2 * TOP_K, :].astype(i32)
    rg_ref[...] = route_g


def _out_proj(hf, hb, gate, of, ob, g2, x2, wout, gn, fg, wr, br):
    t = x2.shape[0]
    tm = min(TM_PROJ, t)
    row = lambda i: (i, 0)
    const = lambda i: (0, 0)
    out_shape = [
        jax.ShapeDtypeStruct((t, D_MODEL), f32),
        jax.ShapeDtypeStruct((t * SLAB, LANES), f32),
        jax.ShapeDtypeStruct((2 * TOP_K, t), i32),
        jax.ShapeDtypeStruct((t, LANES), f32),
        jax.ShapeDtypeStruct((1, LANES), i32),
    ]
    return pl.pallas_call(
        _out_proj_body,
        grid=(t // tm,),
        in_specs=[
            pl.BlockSpec((tm, D_LRU), row),
            pl.BlockSpec((tm, D_LRU), row),
            pl.BlockSpec((tm, D_LRU), row),
            pl.BlockSpec((tm, HV), row),
            pl.BlockSpec((tm, HV), row),
            pl.BlockSpec((tm, HV), row),
            pl.BlockSpec((tm, D_MODEL), row),
            pl.BlockSpec((D_MODEL, D_MODEL), const),
            pl.BlockSpec((1, HV), const),
            pl.BlockSpec((1, D_MODEL), const),
            pl.BlockSpec((2, D_MODEL, LANES), lambda i: (0, 0, 0)),
            pl.BlockSpec((1, LANES), const),
        ],
        out_specs=[
            pl.BlockSpec((tm, D_MODEL), row),
            pl.BlockSpec((tm * SLAB, LANES), row),
            pl.BlockSpec((2 * TOP_K, tm), lambda i: (0, i)),
            pl.BlockSpec((tm, LANES), row),
            pl.BlockSpec((1, LANES), const),
        ],
        out_shape=out_shape,
        scratch_shapes=[pltpu.VMEM((1, LANES), f32)],
        compiler_params=_cparams(1),
        name="out_proj",
    )(hf, hb, gate, of, ob, g2, x2, wout, gn, fg, wr, br)


def _slab_copy(src, src_sub, dst, dst_sub, sem):
    return pltpu.make_async_copy(src.at[pl.ds(src_sub, SLAB)], dst.at[pl.ds(dst_sub, SLAB)], sem)


def _experts_body(be_ref, nreal_ref, src_hbm, dst_hbm, hn_hbm, wg_ref, wu_ref, bgt_ref, bup_ref,
                  wo_ref, bo_ref, y4_hbm, src0, src1, dst0, dst1, xbuf0, xbuf1, ybuf0, ybuf1,
                  isem, dsem, gsem, ssem):
    n = pl.program_id(0)
    n_blocks = pl.num_programs(0)
    bm = xbuf0.shape[0] // SLAB
    spare_sub = y4_hbm.shape[0] - bm * SLAB

    def used(m):
        inside = (m >= 0) & (m < n_blocks)
        return jnp.where(inside, nreal_ref[jnp.clip(m, 0, n_blocks - 1)], 0) > 0

    def src_copy(m, smem, sem):
        return pltpu.make_async_copy(src_hbm.at[jnp.minimum(m, n_blocks - 1)], smem, sem)

    def dst_copy(m, smem, sem):
        return pltpu.make_async_copy(dst_hbm.at[jnp.maximum(m, 0)], smem, sem)

    def issue_gather(smem, xb, sem):
        for r in range(bm):
            _slab_copy(hn_hbm, pl.multiple_of(smem[r], SLAB), xb, r * SLAB, sem).start()

    def wait_gather(xb, sem):
        pltpu.make_async_copy(hn_hbm.at[pl.ds(0, bm * SLAB)], xb, sem).wait()

    def issue_scatter(smem, yb):
        for r in range(bm):
            _slab_copy(yb, r * SLAB, y4_hbm, pl.multiple_of(smem[r], SLAB), ssem).start(priority=1)

    def wait_scatter(yb):
        pltpu.make_async_copy(yb, y4_hbm.at[pl.ds(0, bm * SLAB)], ssem).wait()

    def compute(xb, yb):
        x = _load_slabs(xb).astype(bf16)
        gate = jnp.dot(x, wg_ref[...], preferred_element_type=f32) + bgt_ref[...]
        up = jnp.dot(x, wu_ref[...], preferred_element_type=f32) + bup_ref[...]
        gate = jnp.minimum(gate, SWIGLU_LIMIT)
        up = jnp.clip(up, -SWIGLU_LIMIT, SWIGLU_LIMIT)
        glu = gate * _sigmoid(SWIGLU_ALPHA * gate)
        act = ((up + 1.0) * glu).astype(bf16)
        _store_slabs(yb, jnp.dot(act, wo_ref[...], preferred_element_type=f32) + bo_ref[...])

    bufs = (dict(x_cur=xbuf0, y_cur=ybuf0, g_cur=gsem.at[0], x_alt=xbuf1, y_alt=ybuf1,
                 g_alt=gsem.at[1], src_cur=src0, dst_cur=dst0, src_alt=src1, dst_alt=dst1,
                 i_cur=isem.at[0], d_cur=dsem.at[0], i_alt=isem.at[1], d_alt=dsem.at[1]),
            dict(x_cur=xbuf1, y_cur=ybuf1, g_cur=gsem.at[1], x_alt=xbuf0, y_alt=ybuf0,
                 g_alt=gsem.at[0], src_cur=src1, dst_cur=dst1, src_alt=src0, dst_alt=dst0,
                 i_cur=isem.at[1], d_cur=dsem.at[1], i_alt=isem.at[0], d_alt=dsem.at[0]))

    def step(parity, has_prev):
        b = bufs[parity]
        src_copy(n + 1, b["src_alt"], b["i_alt"]).wait()
        issue_gather(b["src_alt"], b["x_alt"], b["g_alt"])
        if has_prev:
            dst_copy(n - 1, b["dst_alt"], b["d_alt"]).wait()
            issue_scatter(b["dst_alt"], b["y_alt"])
        src_copy(n + 2, b["src_cur"], b["i_cur"]).start()
        dst_copy(n, b["dst_cur"], b["d_cur"]).start()
        wait_gather(b["x_cur"], b["g_cur"])
        compute(b["x_cur"], b["y_cur"])
        if has_prev:
            wait_scatter(b["y_alt"])

    def drain(parity):
        b = bufs[parity]
        src_copy(n + 1, b["src_alt"], b["i_alt"]).wait()
        wait_gather(b["x_cur"], b["g_cur"])
        dst_copy(n - 1, b["dst_alt"], b["d_alt"]).wait()
        issue_scatter(b["dst_alt"], b["y_alt"])
        wait_scatter(b["y_alt"])
        fill = pltpu.make_async_copy(b["y_alt"], y4_hbm.at[pl.ds(spare_sub, bm * SLAB)], ssem)
        fill.start()
        fill.wait()

    @pl.when((n == 0) & used(0))
    def _():
        first = src_copy(0, src0, isem.at[0])
        first.start()
        first.wait()
        src_copy(1, src1, isem.at[1]).start()
        issue_gather(src0, xbuf0, gsem.at[0])
        step(0, False)

    for parity in range(2):
        @pl.when((n > 0) & (n % 2 == parity) & used(n))
        def _():
            step(parity, True)

        @pl.when((n % 2 == parity) & used(n - 1) & jnp.logical_not(used(n)))
        def _():
            drain(parity)


def _experts(block_e, n_real, src_idx, dst_idx, hn, wg, wu, bgt, bup, wo, bo, n_out):
    n_blocks, bm = src_idx.shape
    wmap = lambda n, be, nu: (be[n], 0, 0)
    grid_spec = pltpu.PrefetchScalarGridSpec(
        num_scalar_prefetch=2,
        grid=(n_blocks,),
        in_specs=[
            pl.BlockSpec(memory_space=pl.ANY),
            pl.BlockSpec(memory_space=pl.ANY),
            pl.BlockSpec(memory_space=pl.ANY),
            pl.BlockSpec((None, D_MODEL, D_FF), wmap),
            pl.BlockSpec((None, D_MODEL, D_FF), wmap),
            pl.BlockSpec((None, 1, D_FF), wmap),
            pl.BlockSpec((None, 1, D_FF), wmap),
            pl.BlockSpec((None, D_FF, D_MODEL), wmap),
            pl.BlockSpec((None, 1, D_MODEL), wmap),
        ],
        out_specs=pl.BlockSpec(memory_space=pl.ANY),
        scratch_shapes=(
            [pltpu.SMEM((bm,), i32)] * 4
            + [pltpu.VMEM((bm * SLAB, LANES), f32)] * 4
            + [pltpu.SemaphoreType.DMA((2,))] * 3
            + [pltpu.SemaphoreType.DMA]),
    )
    return pl.pallas_call(
        _experts_body,
        grid_spec=grid_spec,
        out_shape=jax.ShapeDtypeStruct(((n_out + bm) * SLAB, LANES), f32),
        compiler_params=_cparams(1),
        name="experts",
    )(block_e, n_real, src_idx, dst_idx, hn, wg, wu, bgt, bup, wo, bo)


def _combine_body(y0_ref, y1_ref, y2_ref, y3_ref, rg_ref, x1_ref, g_ref, out_ref):
    acc = x1_ref[...]
    for kk, y_ref in enumerate((y0_ref, y1_ref, y2_ref, y3_ref)):
        acc = acc + rg_ref[:, kk:kk + 1] * _load_slabs(y_ref)
    out_ref[...] = _rms(acc, g_ref[...])


def _combine(y4, rg, x1, g):
    t = x1.shape[0]
    tm = min(TM_PROJ, t)
    row = lambda i: (i, 0)
    slot_specs = [pl.BlockSpec((tm * SLAB, LANES),
                               functools.partial(lambda kk, i: (kk * (t // tm) + i, 0), kk))
                  for kk in range(TOP_K)]
    return pl.pallas_call(
        _combine_body,
        grid=(t // tm,),
        in_specs=slot_specs + [
            pl.BlockSpec((tm, LANES), row),
            pl.BlockSpec((tm, D_MODEL), row),
            pl.BlockSpec((1, D_MODEL), lambda i: (0, 0)),
        ],
        out_specs=pl.BlockSpec((tm, D_MODEL), row),
        out_shape=jax.ShapeDtypeStruct((t, D_MODEL), f32),
        compiler_params=_cparams(1),
        name="combine",
    )(y4, y4, y4, y4, rg, x1, g)


REPACK_ROWS = 512
MXU_COLS = 256


def _repack_body(w_ref, gate_ref, up_ref):
    ci = lax.broadcasted_iota(i32, (MXU_COLS, MXU_COLS), 0)
    ji = lax.broadcasted_iota(i32, (MXU_COLS, MXU_COLS), 1)
    src = jnp.where(ji < LANES, 2 * ji, 2 * (ji - LANES) + 1)
    perm = jnp.where(ci == src, 1.0, 0.0).astype(bf16)
    for grp in range(w_ref.shape[1] // MXU_COLS):
        blk = w_ref[:, grp * MXU_COLS:(grp + 1) * MXU_COLS].astype(bf16)
        r = jnp.dot(blk, perm, preferred_element_type=f32)
        gate_ref[:, grp * LANES:(grp + 1) * LANES] = r[:, :LANES].astype(bf16)
        up_ref[:, grp * LANES:(grp + 1) * LANES] = r[:, LANES:].astype(bf16)


def _repack_expert_in(w):
    e, d, two_f = w.shape
    imap = lambda ei, ri: (ei, ri, 0)
    out = jax.ShapeDtypeStruct((e, d, two_f // 2), bf16)
    return pl.pallas_call(
        _repack_body,
        grid=(e, d // REPACK_ROWS),
        in_specs=[pl.BlockSpec((None, REPACK_ROWS, two_f), imap)],
        out_specs=[pl.BlockSpec((None, REPACK_ROWS, two_f // 2), imap)] * 2,
        out_shape=[out, out],
        compiler_params=_cparams(2),
        name="repack",
    )(w)


def _block_diag_dense(w):
    eye = jnp.eye(LRU_BLOCKS, dtype=w.dtype)
    return jnp.einsum('hij,hg->higj', w, eye).reshape(D_LRU, D_LRU)


def _prep(mix_norm_g, w_mix_in, lru_conv_w, lru_conv_b, lru_w_r, lru_b_r, lru_w_i, lru_b_i,
          lru_lambda, gla_w_alpha, gla_b_alpha, gla_norm_g, w_mix_out, ffn_norm_g,
          w_router, b_router, w_exp_in, b_exp_in, w_exp_out, b_exp_out, final_norm_g):
    p = {}
    p["mix_g"] = mix_norm_g[0].reshape(1, D_MODEL)
    p["w_in"] = jnp.pad(w_mix_in[0], ((0, 0), (0, D_IN_PAD - w_mix_in.shape[-1]))).astype(bf16)
    p["conv_w"] = lru_conv_w[0]
    p["conv_b"] = lru_conv_b[0].reshape(1, D_LRU)
    p["wg"] = jnp.stack([
        jnp.concatenate([_block_diag_dense(lru_w_r[0, d]), _block_diag_dense(lru_w_i[0, d])], axis=1)
        for d in range(2)]).astype(bf16)
    p["bg"] = jnp.concatenate([lru_b_r[0], lru_b_i[0]], axis=-1).reshape(2, 1, 2 * D_LRU)
    p["lam"] = lru_lambda[0].reshape(2, 1, D_LRU)
    zeros = jnp.zeros((GLA_RANK, HK), f32)
    p["wa"] = jnp.stack([jnp.concatenate([gla_w_alpha[0, 0], zeros], axis=0),
                         jnp.concatenate([zeros, gla_w_alpha[0, 1]], axis=0)])
    p["ba"] = gla_b_alpha[0].reshape(2, 1, HK)
    p["gn"] = gla_norm_g[0].reshape(1, HV)
    p["w_out"] = w_mix_out[0].astype(bf16)
    p["ffn_g"] = ffn_norm_g[0].reshape(1, D_MODEL)
    w_r = jnp.pad(w_router[0], ((0, 0), (0, LANES - N_EXPERTS)))
    w_r_hi = w_r.astype(bf16)
    p["w_r"] = jnp.stack([w_r_hi, (w_r - w_r_hi.astype(f32)).astype(bf16)])
    p["b_r"] = jnp.pad(b_router[0], (0, LANES - N_EXPERTS), constant_values=-1e30).reshape(1, LANES)
    p["w_gate"], p["w_up"] = _repack_expert_in(w_exp_in[0])
    p["b_gate"] = b_exp_in[0, :, 0::2].reshape(N_EXPERTS, 1, D_FF)
    p["b_up"] = b_exp_in[0, :, 1::2].reshape(N_EXPERTS, 1, D_FF)
    p["w_eo"] = w_exp_out[0].astype(bf16)
    p["b_eo"] = b_exp_out[0].reshape(N_EXPERTS, 1, D_MODEL)
    p["final_g"] = final_norm_g.reshape(1, D_MODEL)
    return p


def _route_tables(route_i, counts, t):
    bm = BM_EXP
    n_assign = t * TOP_K
    n_blocks = n_assign // bm + N_EXPERTS
    cap = n_blocks * bm
    cnt = counts[0, :N_EXPERTS]
    padded = ((cnt + bm - 1) // bm) * bm
    pad_ends = jnp.cumsum(padded)
    pad_starts = pad_ends - padded
    e = route_i[:TOP_K]
    rank = route_i[TOP_K:]
    start_of = jnp.sum(jnp.where(e[..., None] == jnp.arange(N_EXPERTS, dtype=i32), pad_starts, 0),
                       axis=-1)
    pos = (start_of + rank).reshape(-1)
    slot_of_row = jnp.full((cap,), -1, i32).at[pos].set(
        jnp.arange(n_assign, dtype=i32), unique_indices=True)
    real = slot_of_row >= 0
    src_tok = jnp.where(real, slot_of_row % t, 0)
    spare = n_assign + jnp.arange(cap, dtype=i32) % bm
    dst_row = jnp.where(real, slot_of_row, spare)
    src_idx = src_tok.reshape(n_blocks, bm) * SLAB
    dst_idx = dst_row.reshape(n_blocks, bm) * SLAB
    block_start = jnp.arange(n_blocks, dtype=i32) * bm
    block_e = jnp.minimum(jnp.sum(pad_ends[None, :] <= block_start[:, None], axis=1),
                          N_EXPERTS - 1).astype(i32)
    n_real = jnp.clip(cnt[block_e] - (block_start - pad_starts[block_e]), 0, bm).astype(i32)
    return block_e, n_real, src_idx, dst_idx


def _trunk(x, p):
    b, s, _ = x.shape
    t = b * s
    x2 = x.reshape(t, D_MODEL)
    u, gate, q, k, v, g2, al = _in_proj(x2, s, p["mix_g"], p["w_in"], p["conv_w"], p["conv_b"])
    r3 = lambda a: a.reshape(b, s, a.shape[-1])
    hf, hb, of, ob = _mix(r3(u), r3(q), r3(k), r3(v), r3(al),
                          p["wg"], p["bg"], p["lam"], p["wa"], p["ba"])
    f2 = lambda a: a.reshape(t, a.shape[-1])
    x1, hn, route_i, route_g, counts = _out_proj(
        f2(hf), f2(hb), gate, f2(of), f2(ob), g2, x2,
        p["w_out"], p["gn"], p["ffn_g"], p["w_r"], p["b_r"])
    block_e, n_real, src_idx, dst_idx = _route_tables(route_i, counts, t)
    y4 = _experts(block_e, n_real, src_idx, dst_idx, hn, p["w_gate"], p["w_up"], p["b_gate"], p["b_up"],
                  p["w_eo"], p["b_eo"], t * TOP_K)
    y = _combine(y4, route_g, x1, p["final_g"])
    return y.reshape(b, s, D_MODEL)


def kernel(x_prompt, x_sample, mix_norm_g, w_mix_in, lru_conv_w, lru_conv_b, lru_w_r, lru_b_r,
           lru_w_i, lru_b_i, lru_lambda, gla_w_alpha, gla_b_alpha, gla_norm_g, w_mix_out,
           ffn_norm_g, w_router, b_router, w_exp_in, b_exp_in, w_exp_out, b_exp_out,
           final_norm_g):
    p = _prep(mix_norm_g, w_mix_in, lru_conv_w, lru_conv_b, lru_w_r, lru_b_r, lru_w_i, lru_b_i,
              lru_lambda, gla_w_alpha, gla_b_alpha, gla_norm_g, w_mix_out, ffn_norm_g,
              w_router, b_router, w_exp_in, b_exp_in, w_exp_out, b_exp_out, final_norm_g)
    return (_trunk(x_prompt, p), _trunk(x_sample, p))
```

```python
import functools

import jax
import jax.numpy as jnp
from jax import lax
from jax.experimental import pallas as pl
from jax.experimental.pallas import tpu as pltpu

f32 = jnp.float32
bf16 = jnp.bfloat16
i32 = jnp.int32

D_MODEL = 1024
D_LRU = 512
LRU_BLOCKS = 8
LRU_BW = 64
CONV_W = 4
LRU_C = 8.0
GLA_HEADS = 4
GLA_DK = 64
GLA_DV = 128
HK = GLA_HEADS * GLA_DK
HV = GLA_HEADS * GLA_DV
GLA_RANK = 16
GLA_TAU = 16.0
GLA_FAST_LIMIT = 60.0
GLA_SAFE_CHUNK = 16
N_EXPERTS = 32
TOP_K = 4
D_FF = 1024
SWIGLU_LIMIT = 7.0
SWIGLU_ALPHA = 1.702
EPS = 1e-6

LANES = 128
SUBLANES = 8
D_IN_PAD = 2688
VMEM_LIMIT = 56 * 1024 * 1024

TM_PROJ = 512
L_MIX = 256
BM_EXP = 512


def _cparams(n_axes):
    return pltpu.CompilerParams(
        dimension_semantics=("arbitrary",) * n_axes, vmem_limit_bytes=VMEM_LIMIT)


def _rms(x, g):
    ms = jnp.mean(x * x, axis=-1, keepdims=True)
    return x * lax.rsqrt(ms + EPS) * g


def _sigmoid(x):
    return 1.0 / (1.0 + jnp.exp(-x))


SLAB = D_MODEL // LANES


def _store_slabs(ref, x):
    n = x.shape[0]
    for s in range(SLAB):
        ref[pl.ds(s, n, stride=SLAB), :] = x[:, s * LANES:(s + 1) * LANES]


def _load_slabs(ref):
    n = ref.shape[0] // SLAB
    return jnp.concatenate([ref[pl.ds(s, n, stride=SLAB), :] for s in range(SLAB)], axis=1)


def _log_sigmoid(x):
    return jnp.minimum(x, 0.0) - jnp.log1p(jnp.exp(-jnp.abs(x)))


def _in_proj_body(tiles_per_seq, x_ref, xprev_ref, xnext_ref, g_ref, w_ref, cw_ref, cb_ref,
                  u_ref, gate_ref, q_ref, k_ref, v_ref, g2_ref, a_ref, ext_ref):
    i = pl.program_id(0)
    tm = x_ref.shape[0]
    g = g_ref[...]
    xn = _rms(x_ref[...], g).astype(bf16)

    def proj(lo, hi):
        return jnp.dot(xn, w_ref[:, lo:hi], preferred_element_type=f32)

    gate_ref[...] = proj(512, 1024).astype(bf16)
    q_ref[...] = proj(1024, 1280).astype(bf16)
    k_ref[...] = proj(1280, 1536).astype(bf16)
    v_ref[...] = proj(1536, 2048).astype(bf16)
    g2_ref[...] = proj(2048, 2560).astype(bf16)
    a_ref[...] = proj(2560, 2688)[:, :2 * GLA_RANK]

    xh = jnp.concatenate([xprev_ref[...], xnext_ref[...]], axis=0)
    hl = jnp.dot(_rms(xh, g).astype(bf16), w_ref[:, 0:512], preferred_element_type=f32)
    pos = i % tiles_per_seq
    ext_ref[0:SUBLANES, :] = jnp.where(pos == 0, 0.0, hl[:SUBLANES])
    ext_ref[SUBLANES:SUBLANES + tm, :] = proj(0, 512)
    ext_ref[SUBLANES + tm:, :] = jnp.where(pos == tiles_per_seq - 1, 0.0, hl[SUBLANES:])
    u = cb_ref[...]
    for j in range(CONV_W):
        u = u + cw_ref[j:j + 1, :] * ext_ref[pl.ds(SUBLANES - CONV_W // 2 + j, tm), :]
    u_ref[...] = u


def _in_proj(x2, seq, g, w, cw, cb):
    t = x2.shape[0]
    tm = min(TM_PROJ, seq)
    nt = t // tm
    r = tm // SUBLANES
    last_blk = t // SUBLANES - 1
    row = lambda i: (i, 0)
    const = lambda i: (0, 0)
    out_shapes = [
        jax.ShapeDtypeStruct((t, D_LRU), f32),
        jax.ShapeDtypeStruct((t, D_LRU), bf16),
        jax.ShapeDtypeStruct((t, HK), bf16),
        jax.ShapeDtypeStruct((t, HK), bf16),
        jax.ShapeDtypeStruct((t, HV), bf16),
        jax.ShapeDtypeStruct((t, HV), bf16),
        jax.ShapeDtypeStruct((t, 2 * GLA_RANK), f32),
    ]
    return pl.pallas_call(
        functools.partial(_in_proj_body, seq // tm),
        grid=(nt,),
        in_specs=[
            pl.BlockSpec((tm, D_MODEL), row),
            pl.BlockSpec((SUBLANES, D_MODEL), lambda i: (jnp.maximum(i * r - 1, 0), 0)),
            pl.BlockSpec((SUBLANES, D_MODEL), lambda i: (jnp.minimum((i + 1) * r, last_blk), 0)),
            pl.BlockSpec((1, D_MODEL), const),
            pl.BlockSpec((D_MODEL, D_IN_PAD), const),
            pl.BlockSpec((CONV_W, D_LRU), const),
            pl.BlockSpec((1, D_LRU), const),
        ],
        out_specs=[pl.BlockSpec((tm, s.shape[1]), row) for s in out_shapes],
        out_shape=out_shapes,
        scratch_shapes=[pltpu.VMEM((tm + 2 * SUBLANES, D_LRU), f32)],
        compiler_params=_cparams(1),
        name="in_proj",
    )(x2, x2, x2, g, w, cw, cb)


def _lru_tile(u_ref, d, reverse, wg_ref, bg_ref, lam_ref, a_scr, b_scr, hcar_ref, h_out_ref):
    n = u_ref.shape[0]
    u = u_ref[...]
    gates = jnp.dot(u.astype(bf16), wg_ref[d], preferred_element_type=f32) + bg_ref[d]
    r = _sigmoid(gates[:, :D_LRU])
    ig = _sigmoid(gates[:, D_LRU:])
    log_a = r * (LRU_C * _log_sigmoid(lam_ref[d]))
    a = jnp.exp(log_a)
    b = jnp.sqrt(1.0 - a * a) * (ig * u)

    seg = n // SUBLANES
    chunks = range(D_LRU // LANES)
    for c in chunks:
        cols = slice(c * LANES, (c + 1) * LANES)
        for s in range(SUBLANES):
            a_scr[c, pl.ds(s, seg, stride=SUBLANES), :] = a[s * seg:(s + 1) * seg, cols]
            b_scr[c, pl.ds(s, seg, stride=SUBLANES), :] = b[s * seg:(s + 1) * seg, cols]

    acc = [jnp.ones((SUBLANES, LANES), f32) for _ in chunks]
    h = [jnp.zeros((SUBLANES, LANES), f32) for _ in chunks]
    for j in (range(seg - 1, -1, -1) if reverse else range(seg)):
        rows = pl.ds(j * SUBLANES, SUBLANES)
        for c in chunks:
            aj = a_scr[c, rows, :]
            h[c] = aj * h[c] + b_scr[c, rows, :]
            acc[c] = acc[c] * aj
            b_scr[c, rows, :] = h[c]
            a_scr[c, rows, :] = acc[c]

    row = lax.broadcasted_iota(i32, (SUBLANES, LANES), 0)
    h_in = []
    for c in chunks:
        a_tot, h_tot = acc[c], h[c]
        for s in (1, 2, 4):
            if reverse:
                keep = row < SUBLANES - s
                shift = SUBLANES - s
            else:
                keep = row >= s
                shift = s
            a_nb = jnp.where(keep, pltpu.roll(a_tot, shift, 0), 1.0)
            h_nb = jnp.where(keep, pltpu.roll(h_tot, shift, 0), 0.0)
            h_tot = a_tot * h_nb + h_tot
            a_tot = a_tot * a_nb
        carry = hcar_ref[d, c]
        h_edge = a_tot * carry + h_tot
        if reverse:
            h_in.append(jnp.where(row == SUBLANES - 1, carry,
                                  pltpu.roll(h_edge, SUBLANES - 1, 0)))
            edge = h_edge[0:1, :]
        else:
            h_in.append(jnp.where(row == 0, carry, pltpu.roll(h_edge, 1, 0)))
            edge = h_edge[SUBLANES - 1:SUBLANES, :]
        hcar_ref[d, c] = jnp.broadcast_to(edge, (SUBLANES, LANES))

    for j in range(seg):
        rows = pl.ds(j * SUBLANES, SUBLANES)
        for c in chunks:
            b_scr[c, rows, :] = b_scr[c, rows, :] + a_scr[c, rows, :] * h_in[c]
    for c in chunks:
        for s in range(SUBLANES):
            h_out_ref[s * seg:(s + 1) * seg, c * LANES:(c + 1) * LANES] = (
                b_scr[c, pl.ds(s, seg, stride=SUBLANES), :].astype(h_out_ref.dtype))


def _split_bf16(x):
    hi = x.astype(bf16)
    return hi, (x - hi.astype(f32)).astype(bf16)


def _state_diag_mask():
    return (lax.broadcasted_iota(i32, (HK, HV), 0) // GLA_DK
            == lax.broadcasted_iota(i32, (HK, HV), 1) // GLA_DV)


def _gla_fast(q, k, v, la, cum, tot, d, reverse, s_ref, o_out_ref):
    n = q.shape[0]
    ri = lax.broadcasted_iota(i32, (n, n), 0)
    ci = lax.broadcasted_iota(i32, (n, n), 1)
    causal = (ri < ci) if reverse else (ri >= ci)
    qa = q * jnp.exp(cum)
    kb = (k * jnp.exp(-cum)).astype(bf16)
    ke_t = (k * jnp.exp(tot - cum)).T.astype(bf16)

    state = s_ref[d]
    o_inter = jnp.dot(qa.astype(bf16), state.astype(bf16), preferred_element_type=f32)
    head_of_lane = lax.broadcasted_iota(i32, (1, HK), 1) // GLA_DK
    for h in range(GLA_HEADS):
        qh = jnp.where(head_of_lane == h, qa, 0.0).astype(bf16)
        s = lax.dot_general(qh, kb, (((1,), (1,)), ((), ())), preferred_element_type=f32)
        p = jnp.where(causal, s, 0.0).astype(bf16)
        lo, hi = h * GLA_DV, (h + 1) * GLA_DV
        o_h = jnp.dot(p, v[:, lo:hi], preferred_element_type=f32) + o_inter[:, lo:hi]
        o_out_ref[:, lo:hi] = o_h.astype(o_out_ref.dtype)

    kv = jnp.dot(ke_t, v, preferred_element_type=f32)
    dec = jnp.exp(jnp.sum(la.T, axis=1, keepdims=True))
    s_ref[d] = state * dec + jnp.where(_state_diag_mask(), kv, 0.0)


def _gla_safe(q, k, v, la, d, reverse, s_ref, o_out_ref):
    n = q.shape[0]
    c_len = GLA_SAFE_CHUNK
    n_chunks = n // c_len
    ri = lax.broadcasted_iota(i32, (n, n), 0)
    ci = lax.broadcasted_iota(i32, (n, n), 1)
    same = (ri // c_len) == (ci // c_len)
    incl = (ri <= ci) if reverse else (ri >= ci)
    la_hi, la_lo = _split_bf16(la)
    tri = jnp.where(same & incl, 1.0, 0.0).astype(bf16)
    ones = jnp.where(same, 1.0, 0.0).astype(bf16)
    lc = (jnp.dot(tri, la_hi, preferred_element_type=f32)
          + jnp.dot(tri, la_lo, preferred_element_type=f32))
    totc = (jnp.dot(ones, la_hi, preferred_element_type=f32)
            + jnp.dot(ones, la_lo, preferred_element_type=f32))
    vf = v.astype(f32)
    pos = lax.broadcasted_iota(i32, (n, 1), 0) % c_len
    diag = _state_diag_mask()
    head_sum = jnp.where(diag, 1.0, 0.0).astype(bf16)

    o = jnp.zeros((n, HV), f32)
    for delta in range(1 if reverse else 0, c_len):
        if delta == 0:
            k_j, lc_j, v_j = k, lc, vf
        else:
            shift = n - delta if reverse else delta
            k_j = pltpu.roll(k, shift, 0)
            lc_j = pltpu.roll(lc, shift, 0)
            v_j = pltpu.roll(vf, shift, 0)
        valid = (pos + delta < c_len) if reverse else (pos >= delta)
        dec = jnp.exp(jnp.where(valid, lc - lc_j, -1e30))
        scores = jnp.dot((q * k_j * dec).astype(bf16), head_sum, preferred_element_type=f32)
        o = o + scores * v_j

    ql = q * jnp.exp(lc)
    ke_t = (k * jnp.exp(totc - lc)).T.astype(bf16)
    la_t = la.T
    chunk_of_row = lax.broadcasted_iota(i32, (n, 1), 0) // c_len
    chunk_of_col = lax.broadcasted_iota(i32, (1, n), 1) // c_len

    def chunk(step, o_acc):
        c = (n_chunks - 1 - step) if reverse else step
        state = s_ref[d]
        rows = chunk_of_row == c
        qc = jnp.where(rows, ql, 0.0).astype(bf16)
        o_acc = o_acc + jnp.dot(qc, state.astype(bf16), preferred_element_type=f32)
        vc = jnp.where(rows, vf, 0.0).astype(bf16)
        kv = jnp.dot(ke_t, vc, preferred_element_type=f32)
        dec = jnp.exp(jnp.sum(jnp.where(chunk_of_col == c, la_t, 0.0), axis=1, keepdims=True))
        s_ref[d] = state * dec + jnp.where(diag, kv, 0.0)
        return o_acc

    o = lax.fori_loop(0, n_chunks, chunk, o)
    o_out_ref[...] = o.astype(o_out_ref.dtype)


def _gla_tile(q_ref, k_ref, v_ref, al_ref, d, reverse, wa_ref, ba_ref, s_ref, o_out_ref):
    n = q_ref.shape[0]
    z = jnp.dot(al_ref[...], wa_ref[d], preferred_element_type=f32,
                precision=lax.Precision.HIGHEST) + ba_ref[d]
    la = _log_sigmoid(z) * (1.0 / GLA_TAU)
    la_hi, la_lo = _split_bf16(la)
    ri = lax.broadcasted_iota(i32, (n, n), 0)
    ci = lax.broadcasted_iota(i32, (n, n), 1)
    tri = jnp.where((ri <= ci) if reverse else (ri >= ci), 1.0, 0.0).astype(bf16)
    cum = (jnp.dot(tri, la_hi, preferred_element_type=f32)
           + jnp.dot(tri, la_lo, preferred_element_type=f32))
    tot = cum[0:1, :] if reverse else cum[n - 1:n, :]
    q = q_ref[...].astype(f32) * (GLA_DK ** -0.5)
    k = k_ref[...].astype(f32)
    v = v_ref[...]
    strong_decay = jnp.min(tot) < -GLA_FAST_LIMIT

    @pl.when(jnp.logical_not(strong_decay))
    def _():
        _gla_fast(q, k, v, la, cum, tot, d, reverse, s_ref, o_out_ref)

    @pl.when(strong_decay)
    def _():
        _gla_safe(q, k, v, la, d, reverse, s_ref, o_out_ref)


def _mix_body(uf_ref, qf_ref, kf_ref, vf_ref, af_ref, ub_ref, qb_ref, kb_ref, vb_ref, ab_ref,
              wg_ref, bg_ref, lam_ref, wa_ref, ba_ref,
              hf_ref, hb_ref, of_ref, ob_ref,
              a_scr, b_scr, hcar_ref, s_ref):
    @pl.when(pl.program_id(1) == 0)
    def _():
        hcar_ref[...] = jnp.zeros_like(hcar_ref)
        s_ref[...] = jnp.zeros_like(s_ref)

    _lru_tile(uf_ref, 0, False, wg_ref, bg_ref, lam_ref, a_scr, b_scr, hcar_ref, hf_ref)
    _lru_tile(ub_ref, 1, True, wg_ref, bg_ref, lam_ref, a_scr, b_scr, hcar_ref, hb_ref)
    _gla_tile(qf_ref, kf_ref, vf_ref, af_ref, 0, False, wa_ref, ba_ref, s_ref, of_ref)
    _gla_tile(qb_ref, kb_ref, vb_ref, ab_ref, 1, True, wa_ref, ba_ref, s_ref, ob_ref)


def _mix(u, q, k, v, al, wg, bg, lam, wa, ba):
    b, s, _ = u.shape
    n = min(L_MIX, s)
    nt = s // n
    fwd = lambda bi, j: (bi, j, 0)
    bwd = lambda bi, j: (bi, nt - 1 - j, 0)
    const3 = lambda bi, j: (0, 0, 0)

    def tile_specs(imap):
        return [pl.BlockSpec((None, n, c), imap) for c in (D_LRU, HK, HK, HV, 2 * GLA_RANK)]

    out_shape = [jax.ShapeDtypeStruct((b, s, c), bf16) for c in (D_LRU, D_LRU, HV, HV)]
    return pl.pallas_call(
        _mix_body,
        grid=(b, nt),
        in_specs=tile_specs(fwd) + tile_specs(bwd) + [
            pl.BlockSpec(wg.shape, const3),
            pl.BlockSpec(bg.shape, const3),
            pl.BlockSpec(lam.shape, const3),
            pl.BlockSpec(wa.shape, const3),
            pl.BlockSpec(ba.shape, const3),
        ],
        out_specs=[
            pl.BlockSpec((None, n, D_LRU), fwd),
            pl.BlockSpec((None, n, D_LRU), bwd),
            pl.BlockSpec((None, n, HV), fwd),
            pl.BlockSpec((None, n, HV), bwd),
        ],
        out_shape=out_shape,
        scratch_shapes=[
            pltpu.VMEM((D_LRU // LANES, n, LANES), f32),
            pltpu.VMEM((D_LRU // LANES, n, LANES), f32),
            pltpu.VMEM((2, D_LRU // LANES, SUBLANES, LANES), f32),
            pltpu.VMEM((2, HK, HV), f32),
        ],
        compiler_params=_cparams(2),
        name="mix",
    )(u, q, k, v, al, u, q, k, v, al, wg, bg, lam, wa, ba)


def _out_proj_body(hf_ref, hb_ref, gate_ref, of_ref, ob_ref, g2_ref, x_ref,
                   wout_ref, gn_ref, fg_ref, wr_ref, br_ref,
                   x1_ref, hn_ref, ri_ref, rg_ref, cnt_ref, cnt_scr):
    i = pl.program_id(0)
    tm = x_ref.shape[0]

    @pl.when(i == 0)
    def _():
        cnt_scr[...] = jnp.zeros_like(cnt_scr)

    hs = hf_ref[...].astype(f32) + hb_ref[...].astype(f32)
    gt = gate_ref[...].astype(f32)
    gelu = 0.5 * gt * (1.0 + jnp.tanh(0.7978845608028654 * (gt + 0.044715 * (gt * gt * gt))))
    lru_out = (hs * gelu).astype(bf16)

    o = of_ref[...].astype(f32) + ob_ref[...].astype(f32)
    g2 = g2_ref[...].astype(f32)
    silu = g2 * _sigmoid(g2)
    cat = [lru_out]
    for h in range(GLA_HEADS):
        lo, hi = h * GLA_DV, (h + 1) * GLA_DV
        oh = o[:, lo:hi]
        on = oh * lax.rsqrt(jnp.mean(oh * oh, axis=-1, keepdims=True) + EPS)
        cat.append((on * gn_ref[:, lo:hi] * silu[:, lo:hi]).astype(bf16))
    mix = jnp.dot(jnp.concatenate(cat, axis=1), wout_ref[...], preferred_element_type=f32)
    x1 = x_ref[...] + mix
    x1_ref[...] = x1
    hn = _rms(x1, fg_ref[...])
    _store_slabs(hn_ref, hn)

    hn_hi, hn_lo = _split_bf16(hn)
    logits = (jnp.dot(hn_hi, wr_ref[0], preferred_element_type=f32)
              + jnp.dot(hn_hi, wr_ref[1], preferred_element_type=f32)
              + jnp.dot(hn_lo, wr_ref[0], preferred_element_type=f32)) + br_ref[...]
    lane = lax.broadcasted_iota(i32, (tm, LANES), 1)
    lane_f = lane.astype(f32)
    sel_idx, sel_val = [], []
    member = jnp.zeros((tm, LANES), f32)
    for _ in range(TOP_K):
        m = jnp.max(logits, axis=-1, keepdims=True)
        idx = jnp.min(jnp.where(logits == m, lane_f, float(LANES)), axis=-1, keepdims=True)
        hit = lane_f == idx
        member = jnp.where(hit, 1.0, member)
        logits = jnp.where(hit, -jnp.inf, logits)
        sel_idx.append(idx)
        sel_val.append(m)
    ex = [jnp.exp(mv - sel_val[0]) for mv in sel_val]
    inv = 1.0 / (ex[0] + ex[1] + ex[2] + ex[3])

    ri = lax.broadcasted_iota(i32, (tm, tm), 0)
    ci = lax.broadcasted_iota(i32, (tm, tm), 1)
    before = jnp.where(ri > ci, 1.0, 0.0).astype(bf16)
    prefix = jnp.dot(before, member.astype(bf16), preferred_element_type=f32) + cnt_scr[...]
    cnt_new = cnt_scr[...] + jnp.sum(member, axis=0, keepdims=True)
    cnt_scr[...] = cnt_new
    cnt_ref[...] = cnt_new.astype(i32)

    route_i = jnp.zeros((tm, LANES), f32)
    route_g = jnp.zeros((tm, LANES), f32)
    for kk in range(TOP_K):
        rank = jnp.sum(jnp.where(lane_f == sel_idx[kk], prefix, 0.0), axis=-1, keepdims=True)
        route_i = jnp.where(lane == kk, sel_idx[kk], route_i)
        route_i = jnp.where(lane == TOP_K + kk, rank, route_i)
        route_g = jnp.where(lane == kk, ex[kk] * inv, route_g)
    ri_ref[...] = route_i.T[:2 * TOP_K, :].astype(i32)
    rg_ref[...] = route_g


def _out_proj(hf, hb, gate, of, ob, g2, x2, wout, gn, fg, wr, br):
    t = x2.shape[0]
    tm = min(TM_PROJ, t)
    row = lambda i: (i, 0)
    const = lambda i: (0, 0)
    out_shape = [
        jax.ShapeDtypeStruct((t, D_MODEL), f32),
        jax.ShapeDtypeStruct((t * SLAB, LANES), f32),
        jax.ShapeDtypeStruct((2 * TOP_K, t), i32),
        jax.ShapeDtypeStruct((t, LANES), f32),
        jax.ShapeDtypeStruct((1, LANES), i32),
    ]
    return pl.pallas_call(
        _out_proj_body,
        grid=(t // tm,),
        in_specs=[
            pl.BlockSpec((tm, D_LRU), row),
            pl.BlockSpec((tm, D_LRU), row),
            pl.BlockSpec((tm, D_LRU), row),
            pl.BlockSpec((tm, HV), row),
            pl.BlockSpec((tm, HV), row),
            pl.BlockSpec((tm, HV), row),
            pl.BlockSpec((tm, D_MODEL), row),
            pl.BlockSpec((D_MODEL, D_MODEL), const),
            pl.BlockSpec((1, HV), const),
            pl.BlockSpec((1, D_MODEL), const),
            pl.BlockSpec((2, D_MODEL, LANES), lambda i: (0, 0, 0)),
            pl.BlockSpec((1, LANES), const),
        ],
        out_specs=[
            pl.BlockSpec((tm, D_MODEL), row),
            pl.BlockSpec((tm * SLAB, LANES), row),
            pl.BlockSpec((2 * TOP_K, tm), lambda i: (0, i)),
            pl.BlockSpec((tm, LANES), row),
            pl.BlockSpec((1, LANES), const),
        ],
        out_shape=out_shape,
        scratch_shapes=[pltpu.VMEM((1, LANES), f32)],
        compiler_params=_cparams(1),
        name="out_proj",
    )(hf, hb, gate, of, ob, g2, x2, wout, gn, fg, wr, br)


def _slab_copy(src, src_sub, dst, dst_sub, sem):
    return pltpu.make_async_copy(src.at[pl.ds(src_sub, SLAB)], dst.at[pl.ds(dst_sub, SLAB)], sem)


def _experts_body(be_ref, nreal_ref, src_hbm, dst_hbm, hn_hbm, wg_ref, wu_ref, bgt_ref, bup_ref,
                  wo_ref, bo_ref, y4_hbm, src0, src1, dst0, dst1, xbuf0, xbuf1, ybuf0, ybuf1,
                  isem, dsem, gsem, ssem):
    n = pl.program_id(0)
    n_blocks = pl.num_programs(0)
    bm = xbuf0.shape[0] // SLAB
    spare_sub = y4_hbm.shape[0] - bm * SLAB

    def used(m):
        inside = (m >= 0) & (m < n_blocks)
        return jnp.where(inside, nreal_ref[jnp.clip(m, 0, n_blocks - 1)], 0) > 0

    def src_copy(m, smem, sem):
        return pltpu.make_async_copy(src_hbm.at[jnp.minimum(m, n_blocks - 1)], smem, sem)

    def dst_copy(m, smem, sem):
        return pltpu.make_async_copy(dst_hbm.at[jnp.maximum(m, 0)], smem, sem)

    def issue_gather(smem, xb, sem):
        for r in range(bm):
            _slab_copy(hn_hbm, pl.multiple_of(smem[r], SLAB), xb, r * SLAB, sem).start()

    def wait_gather(xb, sem):
        pltpu.make_async_copy(hn_hbm.at[pl.ds(0, bm * SLAB)], xb, sem).wait()

    def issue_scatter(smem, yb):
        for r in range(bm):
            _slab_copy(yb, r * SLAB, y4_hbm, pl.multiple_of(smem[r], SLAB), ssem).start(priority=1)

    def wait_scatter(yb):
        pltpu.make_async_copy(yb, y4_hbm.at[pl.ds(0, bm * SLAB)], ssem).wait()

    def compute(xb, yb):
        x = _load_slabs(xb).astype(bf16)
        gate = jnp.dot(x, wg_ref[...], preferred_element_type=f32) + bgt_ref[...]
        up = jnp.dot(x, wu_ref[...], preferred_element_type=f32) + bup_ref[...]
        gate = jnp.minimum(gate, SWIGLU_LIMIT)
        up = jnp.clip(up, -SWIGLU_LIMIT, SWIGLU_LIMIT)
        glu = gate * _sigmoid(SWIGLU_ALPHA * gate)
        act = ((up + 1.0) * glu).astype(bf16)
        _store_slabs(yb, jnp.dot(act, wo_ref[...], preferred_element_type=f32) + bo_ref[...])

    bufs = (dict(x_cur=xbuf0, y_cur=ybuf0, g_cur=gsem.at[0], x_alt=xbuf1, y_alt=ybuf1,
                 g_alt=gsem.at[1], src_cur=src0, dst_cur=dst0, src_alt=src1, dst_alt=dst1,
                 i_cur=isem.at[0], d_cur=dsem.at[0], i_alt=isem.at[1], d_alt=dsem.at[1]),
            dict(x_cur=xbuf1, y_cur=ybuf1, g_cur=gsem.at[1], x_alt=xbuf0, y_alt=ybuf0,
                 g_alt=gsem.at[0], src_cur=src1, dst_cur=dst1, src_alt=src0, dst_alt=dst0,
                 i_cur=isem.at[1], d_cur=dsem.at[1], i_alt=isem.at[0], d_alt=dsem.at[0]))

    def step(parity, has_prev):
        b = bufs[parity]
        src_copy(n + 1, b["src_alt"], b["i_alt"]).wait()
        issue_gather(b["src_alt"], b["x_alt"], b["g_alt"])
        if has_prev:
            dst_copy(n - 1, b["dst_alt"], b["d_alt"]).wait()
            issue_scatter(b["dst_alt"], b["y_alt"])
        src_copy(n + 2, b["src_cur"], b["i_cur"]).start()
        dst_copy(n, b["dst_cur"], b["d_cur"]).start()
        wait_gather(b["x_cur"], b["g_cur"])
        compute(b["x_cur"], b["y_cur"])
        if has_prev:
            wait_scatter(b["y_alt"])

    def drain(parity):
        b = bufs[parity]
        src_copy(n + 1, b["src_alt"], b["i_alt"]).wait()
        wait_gather(b["x_cur"], b["g_cur"])
        dst_copy(n - 1, b["dst_alt"], b["d_alt"]).wait()
        issue_scatter(b["dst_alt"], b["y_alt"])
        wait_scatter(b["y_alt"])
        fill = pltpu.make_async_copy(b["y_alt"], y4_hbm.at[pl.ds(spare_sub, bm * SLAB)], ssem)
        fill.start()
        fill.wait()

    @pl.when((n == 0) & used(0))
    def _():
        first = src_copy(0, src0, isem.at[0])
        first.start()
        first.wait()
        src_copy(1, src1, isem.at[1]).start()
        issue_gather(src0, xbuf0, gsem.at[0])
        step(0, False)

    for parity in range(2):
        @pl.when((n > 0) & (n % 2 == parity) & used(n))
        def _():
            step(parity, True)

        @pl.when((n % 2 == parity) & used(n - 1) & jnp.logical_not(used(n)))
        def _():
            drain(parity)


def _experts(block_e, n_real, src_idx, dst_idx, hn, wg, wu, bgt, bup, wo, bo, n_out):
    n_blocks, bm = src_idx.shape
    wmap = lambda n, be, nu: (be[n], 0, 0)
    grid_spec = pltpu.PrefetchScalarGridSpec(
        num_scalar_prefetch=2,
        grid=(n_blocks,),
        in_specs=[
            pl.BlockSpec(memory_space=pl.ANY),
            pl.BlockSpec(memory_space=pl.ANY),
            pl.BlockSpec(memory_space=pl.ANY),
            pl.BlockSpec((None, D_MODEL, D_FF), wmap),
            pl.BlockSpec((None, D_MODEL, D_FF), wmap),
            pl.BlockSpec((None, 1, D_FF), wmap),
            pl.BlockSpec((None, 1, D_FF), wmap),
            pl.BlockSpec((None, D_FF, D_MODEL), wmap),
            pl.BlockSpec((None, 1, D_MODEL), wmap),
        ],
        out_specs=pl.BlockSpec(memory_space=pl.ANY),
        scratch_shapes=(
            [pltpu.SMEM((bm,), i32)] * 4
            + [pltpu.VMEM((bm * SLAB, LANES), f32)] * 4
            + [pltpu.SemaphoreType.DMA((2,))] * 3
            + [pltpu.SemaphoreType.DMA]),
    )
    return pl.pallas_call(
        _experts_body,
        grid_spec=grid_spec,
        out_shape=jax.ShapeDtypeStruct(((n_out + bm) * SLAB, LANES), f32),
        compiler_params=_cparams(1),
        name="experts",
    )(block_e, n_real, src_idx, dst_idx, hn, wg, wu, bgt, bup, wo, bo)


def _combine_body(y0_ref, y1_ref, y2_ref, y3_ref, rg_ref, x1_ref, g_ref, out_ref):
    acc = x1_ref[...]
    for kk, y_ref in enumerate((y0_ref, y1_ref, y2_ref, y3_ref)):
        acc = acc + rg_ref[:, kk:kk + 1] * _load_slabs(y_ref)
    out_ref[...] = _rms(acc, g_ref[...])


def _combine(y4, rg, x1, g):
    t = x1.shape[0]
    tm = min(TM_PROJ, t)
    row = lambda i: (i, 0)
    slot_specs = [pl.BlockSpec((tm * SLAB, LANES),
                               functools.partial(lambda kk, i: (kk * (t // tm) + i, 0), kk))
                  for kk in range(TOP_K)]
    return pl.pallas_call(
        _combine_body,
        grid=(t // tm,),
        in_specs=slot_specs + [
            pl.BlockSpec((tm, LANES), row),
            pl.BlockSpec((tm, D_MODEL), row),
            pl.BlockSpec((1, D_MODEL), lambda i: (0, 0)),
        ],
        out_specs=pl.BlockSpec((tm, D_MODEL), row),
        out_shape=jax.ShapeDtypeStruct((t, D_MODEL), f32),
        compiler_params=_cparams(1),
        name="combine",
    )(y4, y4, y4, y4, rg, x1, g)


REPACK_ROWS = 512
MXU_COLS = 256


def _repack_body(w_ref, gate_ref, up_ref):
    ci = lax.broadcasted_iota(i32, (MXU_COLS, MXU_COLS), 0)
    ji = lax.broadcasted_iota(i32, (MXU_COLS, MXU_COLS), 1)
    src = jnp.where(ji < LANES, 2 * ji, 2 * (ji - LANES) + 1)
    perm = jnp.where(ci == src, 1.0, 0.0).astype(bf16)
    for grp in range(w_ref.shape[1] // MXU_COLS):
        blk = w_ref[:, grp * MXU_COLS:(grp + 1) * MXU_COLS].astype(bf16)
        r = jnp.dot(blk, perm, preferred_element_type=f32)
        gate_ref[:, grp * LANES:(grp + 1) * LANES] = r[:, :LANES].astype(bf16)
        up_ref[:, grp * LANES:(grp + 1) * LANES] = r[:, LANES:].astype(bf16)


def _repack_expert_in(w):
    e, d, two_f = w.shape
    imap = lambda ei, ri: (ei, ri, 0)
    out = jax.ShapeDtypeStruct((e, d, two_f // 2), bf16)
    return pl.pallas_call(
        _repack_body,
        grid=(e, d // REPACK_ROWS),
        in_specs=[pl.BlockSpec((None, REPACK_ROWS, two_f), imap)],
        out_specs=[pl.BlockSpec((None, REPACK_ROWS, two_f // 2), imap)] * 2,
        out_shape=[out, out],
        compiler_params=_cparams(2),
        name="repack",
    )(w)


def _block_diag_dense(w):
    eye = jnp.eye(LRU_BLOCKS, dtype=w.dtype)
    return jnp.einsum('hij,hg->higj', w, eye).reshape(D_LRU, D_LRU)


def _prep(mix_norm_g, w_mix_in, lru_conv_w, lru_conv_b, lru_w_r, lru_b_r, lru_w_i, lru_b_i,
          lru_lambda, gla_w_alpha, gla_b_alpha, gla_norm_g, w_mix_out, ffn_norm_g,
          w_router, b_router, w_exp_in, b_exp_in, w_exp_out, b_exp_out, final_norm_g):
    p = {}
    p["mix_g"] = mix_norm_g[0].reshape(1, D_MODEL)
    p["w_in"] = jnp.pad(w_mix_in[0], ((0, 0), (0, D_IN_PAD - w_mix_in.shape[-1]))).astype(bf16)
    p["conv_w"] = lru_conv_w[0]
    p["conv_b"] = lru_conv_b[0].reshape(1, D_LRU)
    p["wg"] = jnp.stack([
        jnp.concatenate([_block_diag_dense(lru_w_r[0, d]), _block_diag_dense(lru_w_i[0, d])], axis=1)
        for d in range(2)]).astype(bf16)
    p["bg"] = jnp.concatenate([lru_b_r[0], lru_b_i[0]], axis=-1).reshape(2, 1, 2 * D_LRU)
    p["lam"] = lru_lambda[0].reshape(2, 1, D_LRU)
    zeros = jnp.zeros((GLA_RANK, HK), f32)
    p["wa"] = jnp.stack([jnp.concatenate([gla_w_alpha[0, 0], zeros], axis=0),
                         jnp.concatenate([zeros, gla_w_alpha[0, 1]], axis=0)])
    p["ba"] = gla_b_alpha[0].reshape(2, 1, HK)
    p["gn"] = gla_norm_g[0].reshape(1, HV)
    p["w_out"] = w_mix_out[0].astype(bf16)
    p["ffn_g"] = ffn_norm_g[0].reshape(1, D_MODEL)
    w_r = jnp.pad(w_router[0], ((0, 0), (0, LANES - N_EXPERTS)))
    w_r_hi = w_r.astype(bf16)
    p["w_r"] = jnp.stack([w_r_hi, (w_r - w_r_hi.astype(f32)).astype(bf16)])
    p["b_r"] = jnp.pad(b_router[0], (0, LANES - N_EXPERTS), constant_values=-1e30).reshape(1, LANES)
    p["w_gate"], p["w_up"] = _repack_expert_in(w_exp_in[0])
    p["b_gate"] = b_exp_in[0, :, 0::2].reshape(N_EXPERTS, 1, D_FF)
    p["b_up"] = b_exp_in[0, :, 1::2].reshape(N_EXPERTS, 1, D_FF)
    p["w_eo"] = w_exp_out[0].astype(bf16)
    p["b_eo"] = b_exp_out[0].reshape(N_EXPERTS, 1, D_MODEL)
    p["final_g"] = final_norm_g.reshape(1, D_MODEL)
    return p


def _route_tables(route_i, counts, t):
    bm = BM_EXP
    n_assign = t * TOP_K
    n_blocks = n_assign // bm + N_EXPERTS
    cap = n_blocks * bm
    cnt = counts[0, :N_EXPERTS]
    padded = ((cnt + bm - 1) // bm) * bm
    pad_ends = jnp.cumsum(padded)
    pad_starts = pad_ends - padded
    e = route_i[:TOP_K]
    rank = route_i[TOP_K:]
    start_of = jnp.sum(jnp.where(e[..., None] == jnp.arange(N_EXPERTS, dtype=i32), pad_starts, 0),
                       axis=-1)
    pos = (start_of + rank).reshape(-1)
    slot_of_row = jnp.full((cap,), -1, i32).at[pos].set(
        jnp.arange(n_assign, dtype=i32), unique_indices=True)
    real = slot_of_row >= 0
    src_tok = jnp.where(real, slot_of_row % t, 0)
    spare = n_assign + jnp.arange(cap, dtype=i32) % bm
    dst_row = jnp.where(real, slot_of_row, spare)
    src_idx = src_tok.reshape(n_blocks, bm) * SLAB
    dst_idx = dst_row.reshape(n_blocks, bm) * SLAB
    block_start = jnp.arange(n_blocks, dtype=i32) * bm
    block_e = jnp.minimum(jnp.sum(pad_ends[None, :] <= block_start[:, None], axis=1),
                          N_EXPERTS - 1).astype(i32)
    n_real = jnp.clip(cnt[block_e] - (block_start - pad_starts[block_e]), 0, bm).astype(i32)
    return block_e, n_real, src_idx, dst_idx


def _trunk(x, p):
    b, s, _ = x.shape
    t = b * s
    x2 = x.reshape(t, D_MODEL)
    u, gate, q, k, v, g2, al = _in_proj(x2, s, p["mix_g"], p["w_in"], p["conv_w"], p["conv_b"])
    r3 = lambda a: a.reshape(b, s, a.shape[-1])
    hf, hb, of, ob = _mix(r3(u), r3(q), r3(k), r3(v), r3(al),
                          p["wg"], p["bg"], p["lam"], p["wa"], p["ba"])
    f2 = lambda a: a.reshape(t, a.shape[-1])
    x1, hn, route_i, route_g, counts = _out_proj(
        f2(hf), f2(hb), gate, f2(of), f2(ob), g2, x2,
        p["w_out"], p["gn"], p["ffn_g"], p["w_r"], p["b_r"])
    block_e, n_real, src_idx, dst_idx = _route_tables(route_i, counts, t)
    y4 = _experts(block_e, n_real, src_idx, dst_idx, hn, p["w_gate"], p["w_up"], p["b_gate"], p["b_up"],
                  p["w_eo"], p["b_eo"], t * TOP_K)
    y = _combine(y4, route_g, x1, p["final_g"])
    return y.reshape(b, s, D_MODEL)


def kernel(x_prompt, x_sample, mix_norm_g, w_mix_in, lru_conv_w, lru_conv_b, lru_w_r, lru_b_r,
           lru_w_i, lru_b_i, lru_lambda, gla_w_alpha, gla_b_alpha, gla_norm_g, w_mix_out,
           ffn_norm_g, w_router, b_router, w_exp_in, b_exp_in, w_exp_out, b_exp_out,
           final_norm_g):
    p = _prep(mix_norm_g, w_mix_in, lru_conv_w, lru_conv_b, lru_w_r, lru_b_r, lru_w_i, lru_b_i,
              lru_lambda, gla_w_alpha, gla_b_alpha, gla_norm_g, w_mix_out, ffn_norm_g,
              w_router, b_router, w_exp_in, b_exp_in, w_exp_out, b_exp_out, final_norm_g)
    return (_trunk(x_prompt, p), _trunk(x_sample, p))
```
